```python
import math
import jax, jax.numpy as jnp
from jax import lax
import numpy as np

D_MODEL = 1024
BATCH = 16
SEQ = 2048
DEPTH = 4
DEC_BATCH = 16
DEC_SEQ = 64
PAST_LEN = 4096

CHUNK = 64
NH_A = 4
D_A = 1024
DH_A = D_A // NH_A
D_B = 1024
CONV_W = 31
D_FF = 2816
FFN_CONV_W = 3
EPS = 1e-6
SPLITS = [D_A, 2 * D_A, 3 * D_A, 4 * D_A, 4 * D_A + NH_A, 4 * D_A + 2 * NH_A, 4 * D_A + 2 * NH_A + 2 * D_B]
N_IN = 4 * D_A + 2 * NH_A + 2 * D_B + 2 * D_MODEL

kernel_name = 'mlstm_conformer_hybrid_stream'


def rmsnorm(x, g):
    xf = x.astype(jnp.float32)
    y = xf * lax.rsqrt(jnp.mean(xf * xf, axis=-1, keepdims=True) + EPS)
    return (y * g.astype(jnp.float32)).astype(x.dtype)


def layernorm(x, g, b):
    xf = x.astype(jnp.float32)
    mu = jnp.mean(xf, axis=-1, keepdims=True)
    xc = xf - mu
    y = xc * lax.rsqrt(jnp.mean(xc * xc, axis=-1, keepdims=True) + EPS)
    return (y * g.astype(jnp.float32) + b.astype(jnp.float32)).astype(x.dtype)


def causal_dwconv(x, buf, w, b):
    width = w.shape[0]
    xp = jnp.concatenate([buf.astype(x.dtype), x], axis=1)
    y = lax.conv_general_dilated(xp, w[:, None, :].astype(x.dtype), window_strides=(1,), padding='VALID',
                                 dimension_numbers=('NWC', 'WIO', 'NWC'), feature_group_count=x.shape[-1])
    new_buf = xp[:, xp.shape[1] - (width - 1):]
    return y + b.astype(x.dtype), new_buf


def mlstm_chunk(carry, inp):
    C, n, m = carry
    q, k, v, ig, lf = inp
    L = q.shape[1]
    bcum = jnp.cumsum(lf, axis=1).transpose(0, 2, 1)
    igt = ig.transpose(0, 2, 1)
    logd = bcum[:, :, :, None] - bcum[:, :, None, :] + igt[:, :, None, :]
    causal = jnp.tril(jnp.ones((L, L), dtype=bool))
    logd = jnp.where(causal, logd, -jnp.inf)
    g = bcum + m[:, :, None]
    m_tok = jnp.maximum(g, jnp.max(logd, axis=-1))
    w = jnp.exp(logd - m_tok[..., None])
    inter = jnp.exp(g - m_tok)
    s = jnp.einsum('bjhd,bshd->bhjs', q, k) * w
    inter_blh = inter.transpose(0, 2, 1)
    num = jnp.einsum('bhjs,bshe->bjhe', s, v) + inter_blh[..., None] * jnp.einsum('bjhd,bhde->bjhe', q, C)
    den = jnp.sum(s, axis=-1).transpose(0, 2, 1) + inter_blh * jnp.einsum('bjhd,bhd->bjh', q, n)
    floor = jnp.exp(-m_tok).transpose(0, 2, 1)
    h = num / jnp.maximum(jnp.abs(den), floor)[..., None]
    m_new = m_tok[:, :, -1]
    decay = inter[:, :, -1]
    w_last = w[:, :, -1, :]
    C_new = decay[..., None, None] * C + jnp.einsum('bhs,bshd,bshe->bhde', w_last, k, v)
    n_new = decay[..., None] * n + jnp.einsum('bhs,bshd->bhd', w_last, k)
    return (C_new, n_new, m_new), h


def mlstm(q, k, v, ig, lf, C, n, m):
    B, T, H, D = q.shape
    L = CHUNK if T % CHUNK == 0 else T
    nc = T // L

    def split(a):
        return a.reshape((B, nc, L) + a.shape[2:]).swapaxes(0, 1)

    xs = (split(q), split(k), split(v), split(ig), split(lf))
    (C, n, m), h = lax.scan(mlstm_chunk, (C, n, m), xs)
    h = h.swapaxes(0, 1).reshape(B, T, H, D)
    return h, C, n, m


def layer(x, C, n, m, conv_buf, ffn_buf, p):
    B, T, _ = x.shape
    xn = rmsnorm(x, p['norm_mix_pre'])
    proj = xn @ p['w_in']
    q, k, v, o, ig, fg, u, gates = jnp.split(proj, SPLITS, axis=-1)
    f32 = jnp.float32
    q = q.reshape(B, T, NH_A, DH_A).astype(f32)
    k = k.reshape(B, T, NH_A, DH_A).astype(f32) * (1.0 / math.sqrt(DH_A))
    v = v.reshape(B, T, NH_A, DH_A).astype(f32)
    ig = ig.astype(f32) + p['b_i'].astype(f32)
    lf = jax.nn.log_sigmoid(fg.astype(f32) + p['b_f'].astype(f32))
    h, C, n, m = mlstm(q, k, v, ig, lf, C.astype(f32), n.astype(f32), m.astype(f32))
    h = h * lax.rsqrt(jnp.mean(h * h, axis=-1, keepdims=True) + EPS)
    h = (h.reshape(B, T, D_A) * p['mlstm_norm'].astype(f32)).astype(x.dtype)
    ya = (h * jax.nn.sigmoid(o)) @ p['w_proj_a']
    glu = u[..., :D_B] * jax.nn.sigmoid(u[..., D_B:])
    c, conv_buf = causal_dwconv(glu, conv_buf, p['conv_dw'], p['conv_b'])
    c = jax.nn.silu(layernorm(c, p['conv_ln_g'], p['conv_ln_b']))
    yb = c @ p['w_proj_b'] + p['b_proj_b']
    gt = jax.nn.sigmoid(gates + p['b_merge'])
    mix = (gt[..., :D_MODEL] * ya + gt[..., D_MODEL:] * yb) @ p['w_out']
    x = x + rmsnorm(mix, p['norm_mix_post'])
    hn = rmsnorm(x, p['norm_ffn_pre'])
    up = hn @ p['w_up']
    up, ffn_buf = causal_dwconv(up, ffn_buf, p['ffn_dw'], p['ffn_dw_b'])
    f = (jax.nn.gelu(up[..., :D_FF]) * up[..., D_FF:]) @ p['w_down']
    x = x + rmsnorm(f, p['norm_ffn_post'])
    return x, C, n, m, conv_buf, ffn_buf


def setup_inputs(seed: int = 0) -> dict:
    key = jax.random.key(seed)
    ks = jax.random.split(key, 32)
    nrm = jax.random.normal
    f32 = jnp.float32

    def gain(k, shape):
        return 1.0 + 0.05 * nrm(k, shape, f32)

    return {
        'x_prompt': nrm(ks[0], (BATCH, SEQ, D_MODEL), f32),
        'x_sample': nrm(ks[1], (DEC_BATCH, DEC_SEQ, D_MODEL), f32),
        'state_mlstm_C': 0.05 * nrm(ks[2], (DEPTH, DEC_BATCH, NH_A, DH_A, DH_A), f32),
        'state_mlstm_n': 0.1 * nrm(ks[3], (DEPTH, DEC_BATCH, NH_A, DH_A), f32),
        'state_mlstm_m': jax.random.uniform(ks[4], (DEPTH, DEC_BATCH, NH_A), f32, 0.0, 2.0),
        'cache_conv': 0.5 * nrm(ks[5], (DEPTH, DEC_BATCH, CONV_W - 1, D_B), f32),
        'cache_ffn_conv': nrm(ks[6], (DEPTH, DEC_BATCH, FFN_CONV_W - 1, 2 * D_FF), f32),
        'norm_mix_pre': gain(ks[7], (DEPTH, D_MODEL)),
        'norm_mix_post': gain(ks[8], (DEPTH, D_MODEL)),
        'norm_ffn_pre': gain(ks[9], (DEPTH, D_MODEL)),
        'norm_ffn_post': gain(ks[10], (DEPTH, D_MODEL)),
        'w_in': nrm(ks[11], (DEPTH, D_MODEL, N_IN), f32) * D_MODEL ** -0.5,
        'b_i': 0.1 * nrm(ks[12], (DEPTH, NH_A), f32),
        'b_f': jnp.linspace(3.0, 6.0, NH_A, dtype=f32)[None, :] + 0.1 * nrm(ks[13], (DEPTH, NH_A), f32),
        'mlstm_norm': gain(ks[14], (DEPTH, D_A)),
        'w_proj_a': nrm(ks[15], (DEPTH, D_A, D_MODEL), f32) * D_A ** -0.5,
        'conv_dw': nrm(ks[16], (DEPTH, CONV_W, D_B), f32) * CONV_W ** -0.5,
        'conv_b': 0.02 * nrm(ks[17], (DEPTH, D_B), f32),
        'conv_ln_g': gain(ks[18], (DEPTH, D_B)),
        'conv_ln_b': 0.02 * nrm(ks[19], (DEPTH, D_B), f32),
        'w_proj_b': nrm(ks[20], (DEPTH, D_B, D_MODEL), f32) * D_B ** -0.5,
        'b_proj_b': 0.02 * nrm(ks[21], (DEPTH, D_MODEL), f32),
        'b_merge': 0.1 * nrm(ks[22], (DEPTH, 2 * D_MODEL), f32),
        'w_out': nrm(ks[23], (DEPTH, D_MODEL, D_MODEL), f32) * D_MODEL ** -0.5,
        'w_up': nrm(ks[24], (DEPTH, D_MODEL, 2 * D_FF), f32) * D_MODEL ** -0.5,
        'ffn_dw': nrm(ks[25], (DEPTH, FFN_CONV_W, 2 * D_FF), f32) * FFN_CONV_W ** -0.5,
        'ffn_dw_b': 0.02 * nrm(ks[26], (DEPTH, 2 * D_FF), f32),
        'w_down': nrm(ks[27], (DEPTH, D_FF, D_MODEL), f32) * D_FF ** -0.5,
    }


def reference(x_prompt, x_sample, state_mlstm_C, state_mlstm_n, state_mlstm_m, cache_conv, cache_ffn_conv,
              norm_mix_pre, norm_mix_post, norm_ffn_pre, norm_ffn_post, w_in, b_i, b_f, mlstm_norm, w_proj_a,
              conv_dw, conv_b, conv_ln_g, conv_ln_b, w_proj_b, b_proj_b, b_merge, w_out, w_up, ffn_dw, ffn_dw_b,
              w_down):
    Bp = x_prompt.shape[0]
    yp = x_prompt
    ys = x_sample
    pC, pn, pm, pconv, pffn = [], [], [], [], []
    sC, sn, sm, sconv, sffn = [], [], [], [], []
    for l in range(DEPTH):
        p = {
            'norm_mix_pre': norm_mix_pre[l], 'norm_mix_post': norm_mix_post[l],
            'norm_ffn_pre': norm_ffn_pre[l], 'norm_ffn_post': norm_ffn_post[l],
            'w_in': w_in[l], 'b_i': b_i[l], 'b_f': b_f[l], 'mlstm_norm': mlstm_norm[l],
            'w_proj_a': w_proj_a[l], 'conv_dw': conv_dw[l], 'conv_b': conv_b[l],
            'conv_ln_g': conv_ln_g[l], 'conv_ln_b': conv_ln_b[l], 'w_proj_b': w_proj_b[l],
            'b_proj_b': b_proj_b[l], 'b_merge': b_merge[l], 'w_out': w_out[l], 'w_up': w_up[l],
            'ffn_dw': ffn_dw[l], 'ffn_dw_b': ffn_dw_b[l], 'w_down': w_down[l],
        }
        C0 = jnp.zeros((Bp, NH_A, DH_A, DH_A), jnp.float32)
        n0 = jnp.zeros((Bp, NH_A, DH_A), jnp.float32)
        m0 = jnp.zeros((Bp, NH_A), jnp.float32)
        cb0 = jnp.zeros((Bp, CONV_W - 1, D_B), x_prompt.dtype)
        fb0 = jnp.zeros((Bp, FFN_CONV_W - 1, 2 * D_FF), x_prompt.dtype)
        yp, C1, n1, m1, cb1, fb1 = layer(yp, C0, n0, m0, cb0, fb0, p)
        pC.append(C1); pn.append(n1); pm.append(m1); pconv.append(cb1); pffn.append(fb1)
        ys, C2, n2, m2, cb2, fb2 = layer(ys, state_mlstm_C[l], state_mlstm_n[l], state_mlstm_m[l],
                                         cache_conv[l], cache_ffn_conv[l], p)
        sC.append(C2); sn.append(n2); sm.append(m2); sconv.append(cb2); sffn.append(fb2)
    return (yp, ys,
            jnp.stack(pC), jnp.stack(pn), jnp.stack(pm), jnp.stack(pconv), jnp.stack(pffn),
            jnp.stack(sC), jnp.stack(sn), jnp.stack(sm), jnp.stack(sconv), jnp.stack(sffn))
```

```python
import functools
import math

import jax
import jax.numpy as jnp
from jax import lax
from jax.experimental import pallas as pl
from jax.experimental.pallas import tpu as pltpu

EPS = 1e-6
NUM_HEADS = 4
SUBLANES = 8
LANES = 128
CONV_HIST_ROWS = 32
FFN_HIST_ROWS = 8
MAX_BLOCK_ROWS = 256
VMEM_LIMIT_BYTES = 56 * 1024 * 1024

F32 = jnp.float32
BF16 = jnp.bfloat16


def _dot(a, b):
    return jnp.dot(a, b, preferred_element_type=F32)


def _dot_nt(a, b):
    return lax.dot_general(a, b, (((1,), (1,)), ((), ())), preferred_element_type=F32)


def _dot_tn(a, b):
    return lax.dot_general(a, b, (((0,), (0,)), ((), ())), preferred_element_type=F32)


def _sigmoid(x):
    return 1.0 / (1.0 + jnp.exp(-x))


def _log_sigmoid(x):
    return jnp.minimum(x, 0.0) - jnp.log1p(jnp.exp(-jnp.abs(x)))


def _rms_scale(x):
    return x * lax.rsqrt(jnp.mean(x * x, axis=-1, keepdims=True) + EPS)


def _split_bf16(x):
    hi = x.astype(BF16)
    lo = (x - hi.astype(F32)).astype(BF16)
    return hi, lo


def _mixer_kernel(*refs, nb, tt, has_init):
    if has_init:
        (x_ref, c0_ref, n0_ref, m0_ref, cb0_ref, *rest) = refs
    else:
        (x_ref, *rest) = refs
    (g_pre_ref, w_qkvo_ref, w_if_ref, w_ift_ref, b_if_row_ref, b_if_col_ref, mnorm_ref, w_pa_ref,
     w_u_ref, conv_w_ref, conv_b_ref, ln_g_ref, ln_b_ref, w_pb_ref, b_pb_ref, w_g_ref, b_merge_ref,
     w_out_ref, g_post_ref,
     y_ref, c_ref, n_ref, m_ref, cb_ref,
     xn_scr, hg_scr, glu_scr, conv_scr) = rest

    t = pl.program_id(1)
    rows = nb * tt
    d_model = x_ref.shape[-1]
    d_a = w_pa_ref.shape[0]
    dh = d_a // NUM_HEADS
    d_b = w_pb_ref.shape[0]
    conv_w = conv_w_ref.shape[0]
    hist0 = CONV_HIST_ROWS - (conv_w - 1)

    @pl.when(t == 0)
    def _init_state():
        glu_scr[:, 0:CONV_HIST_ROWS, :] = jnp.zeros((nb, CONV_HIST_ROWS, d_b), F32)
        if has_init:
            c_ref[...] = c0_ref[...]
            n_ref[...] = n0_ref[...]
            m_ref[...] = m0_ref[...]
            glu_scr[:, hist0:CONV_HIST_ROWS, :] = cb0_ref[...]
        else:
            c_ref[...] = jnp.zeros(c_ref.shape, F32)
            n_ref[...] = jnp.zeros(n_ref.shape, F32)
            m_ref[...] = jnp.zeros(m_ref.shape, F32)

    x = x_ref[...].reshape(rows, d_model)
    xn_scr[...] = (_rms_scale(x) * g_pre_ref[...]).astype(BF16)

    row_id = lax.broadcasted_iota(jnp.int32, (tt, tt), 0)
    col_id = lax.broadcasted_iota(jnp.int32, (tt, tt), 1)
    causal = row_id >= col_id
    tril = jnp.where(causal, 1.0, 0.0).astype(BF16)
    triu = jnp.where(col_id >= row_id, 1.0, 0.0).astype(BF16)

    gates = []
    for b in range(nb):
        xb = xn_scr[b * tt:(b + 1) * tt, :]
        g_col = _dot(xb, w_if_ref[...]) + b_if_row_ref[...]
        lf_col = _log_sigmoid(g_col)
        hi, lo = _split_bf16(lf_col)
        bcum_col = _dot(tril, hi) + _dot(tril, lo)
        g_row = _dot_nt(w_ift_ref[...], xb) + b_if_col_ref[:, 0:1]
        lf_row = _log_sigmoid(g_row)
        hi, lo = _split_bf16(lf_row)
        bcum_row = _dot(hi, triu) + _dot(lo, triu)
        gates.append((g_col, bcum_col, g_row, bcum_row))

    for h in range(NUM_HEADS):
        xn = xn_scr[...]
        q_all = _dot(xn, w_qkvo_ref[:, h * dh:(h + 1) * dh])
        k_all = _dot(xn, w_qkvo_ref[:, d_a + h * dh:d_a + (h + 1) * dh]) * (1.0 / math.sqrt(dh))
        v_all = _dot(xn, w_qkvo_ref[:, 2 * d_a + h * dh:2 * d_a + (h + 1) * dh])
        o_all = _dot(xn, w_qkvo_ref[:, 3 * d_a + h * dh:3 * d_a + (h + 1) * dh])
        for b in range(nb):
            g_col, bcum_col, g_row, bcum_row = gates[b]
            rs = slice(b * tt, (b + 1) * tt)
            q, k, v, o = q_all[rs], k_all[rs], v_all[rs], o_all[rs]
            qb, kb, vb = q.astype(BF16), k.astype(BF16), v.astype(BF16)
            c_prev = c_ref[b, h]
            n_prev = n_ref[b, h:h + 1, :]
            m_prev = m_ref[b, h:h + 1, 0:1]

            ig_c = g_col[:, h:h + 1]
            bc = bcum_col[:, NUM_HEADS + h:NUM_HEADS + h + 1]
            ig_r = g_row[h:h + 1, :]
            br = bcum_row[NUM_HEADS + h:NUM_HEADS + h + 1, :]

            logd = jnp.where(causal, (bc - br) + ig_r, -jnp.inf)
            g = bc + m_prev
            m_tok = jnp.maximum(g, jnp.max(logd, axis=-1, keepdims=True))
            w = jnp.exp(logd - m_tok)
            inter = jnp.exp(g - m_tok)
            s = _dot_nt(qb, kb) * w
            num = _dot(s.astype(BF16), vb) + inter * _dot(qb, c_prev.astype(BF16))
            den = jnp.sum(s, axis=-1, keepdims=True) + inter * jnp.sum(q * n_prev, axis=-1, keepdims=True)
            hh = num * (1.0 / jnp.maximum(jnp.abs(den), jnp.exp(-m_tok)))

            m_new = m_tok[tt - 1:tt, :]
            decay = inter[tt - 1:tt, :]
            w_last = jnp.exp((bc[tt - 1:tt, :] - bc) + ig_c - m_new)
            c_ref[b, h] = decay * c_prev + _dot_tn(kb, (w_last * v).astype(BF16))
            n_ref[b, h:h + 1, :] = decay * n_prev + jnp.sum(w_last * k, axis=0, keepdims=True)
            m_ref[b, h:h + 1, :] = jnp.broadcast_to(m_new, (1, LANES))

            hn = _rms_scale(hh) * mnorm_ref[:, h * dh:(h + 1) * dh]
            hg_scr[rs, h * dh:(h + 1) * dh] = (hn * _sigmoid(o)).astype(BF16)

    ya = _dot(hg_scr[...], w_pa_ref[...])

    u = _dot(xn_scr[...], w_u_ref[...])
    glu = u[:, :d_b] * _sigmoid(u[:, d_b:])
    glu_scr[:, CONV_HIST_ROWS:CONV_HIST_ROWS + tt, :] = glu.reshape(nb, tt, d_b)
    for b in range(nb):
        for j in range(d_b // LANES):
            ls = slice(j * LANES, (j + 1) * LANES)
            acc = jnp.broadcast_to(conv_b_ref[:, ls], (tt, LANES))
            for kk in range(conv_w):
                acc = acc + conv_w_ref[kk:kk + 1, ls] * glu_scr[b, hist0 + kk:hist0 + kk + tt, ls]
            conv_scr[b * tt:(b + 1) * tt, ls] = acc
    glu_scr[:, 0:CONV_HIST_ROWS, :] = glu_scr[:, tt:tt + CONV_HIST_ROWS, :]

    c = conv_scr[...]
    xc = c - jnp.mean(c, axis=-1, keepdims=True)
    cn = xc * lax.rsqrt(jnp.mean(xc * xc, axis=-1, keepdims=True) + EPS) * ln_g_ref[...] + ln_b_ref[...]
    act = cn * _sigmoid(cn)
    yb = _dot(act.astype(BF16), w_pb_ref[...]) + b_pb_ref[...]

    gt = _sigmoid(_dot(xn_scr[...], w_g_ref[...]) + b_merge_ref[...])
    mix = gt[:, :d_model] * ya + gt[:, d_model:] * yb
    mo = _dot(mix.astype(BF16), w_out_ref[...])
    y = x_ref[...].reshape(rows, d_model) + _rms_scale(mo) * g_post_ref[...]
    y_ref[...] = y.reshape(nb, tt, d_model)

    @pl.when(t == pl.num_programs(1) - 1)
    def _emit_conv_state():
        cb_ref[...] = glu_scr[:, hist0:CONV_HIST_ROWS, :]


def _ffn_kernel(*refs, nb, tt, has_init):
    if has_init:
        (x_ref, fb0_ref, *rest) = refs
    else:
        (x_ref, *rest) = refs
    (g_pre_ref, w_up_ref, dw_ref, dw_b_ref, w_down_ref, g_post_ref,
     y_ref, fb_ref, up_scr) = rest

    t = pl.program_id(1)
    rows = nb * tt
    d_model = x_ref.shape[-1]
    d_ff = w_down_ref.shape[0]
    width = dw_ref.shape[0]
    hist0 = FFN_HIST_ROWS - (width - 1)

    @pl.when(t == 0)
    def _init_state():
        up_scr[:, 0:FFN_HIST_ROWS, :] = jnp.zeros((nb, FFN_HIST_ROWS, 2 * d_ff), F32)
        if has_init:
            up_scr[:, hist0:FFN_HIST_ROWS, :] = fb0_ref[...]

    x = x_ref[...].reshape(rows, d_model)
    hn = (_rms_scale(x) * g_pre_ref[...]).astype(BF16)
    up = _dot(hn, w_up_ref[...])
    up_scr[:, FFN_HIST_ROWS:FFN_HIST_ROWS + tt, :] = up.reshape(nb, tt, 2 * d_ff)

    conv = jnp.broadcast_to(dw_b_ref[...], (nb, tt, 2 * d_ff))
    for kk in range(width):
        conv = conv + dw_ref[kk:kk + 1, :] * up_scr[:, hist0 + kk:hist0 + kk + tt, :]
    up_scr[:, 0:FFN_HIST_ROWS, :] = up_scr[:, tt:tt + FFN_HIST_ROWS, :]
    conv = conv.reshape(rows, 2 * d_ff)

    gate = conv[:, :d_ff]
    cdf = 0.5 * (1.0 + jnp.tanh(math.sqrt(2.0 / math.pi) * (gate + 0.044715 * (gate * gate * gate))))
    f = _dot((gate * cdf * conv[:, d_ff:]).astype(BF16), w_down_ref[...])
    y = x_ref[...].reshape(rows, d_model) + _rms_scale(f) * g_post_ref[...]
    y_ref[...] = y.reshape(nb, tt, d_model)

    @pl.when(t == pl.num_programs(1) - 1)
    def _emit_conv_state():
        fb_ref[...] = up_scr[:, hist0:FFN_HIST_ROWS, :]


def _block_rows(batch, seq):
    tt = MAX_BLOCK_ROWS if seq % MAX_BLOCK_ROWS == 0 else seq
    assert tt % (2 * SUBLANES) == 0 and tt >= CONV_HIST_ROWS, (batch, seq)
    nb = max(1, min(batch, MAX_BLOCK_ROWS // tt))
    while batch % nb:
        nb -= 1
    return nb, tt


def _resident(shape):
    return pl.BlockSpec(shape, lambda b, t: (0,) * len(shape), pipeline_mode=pl.Buffered(1))


def _per_batch(shape, nb):
    return pl.BlockSpec((nb,) + tuple(shape[1:]), lambda b, t: (b,) + (0,) * (len(shape) - 1))


def _mixer_layer(x, init, p):
    batch, seq, d_model = x.shape
    nb, tt = _block_rows(batch, seq)
    d_a = p['w_pa'].shape[0]
    dh = d_a // NUM_HEADS
    d_b = p['w_pb'].shape[0]
    conv_w = p['conv_w'].shape[0]
    has_init = init is not None
    x_spec = pl.BlockSpec((nb, tt, d_model), lambda b, t: (b, t, 0))
    weights = [p[k] for k in ('g_pre', 'w_qkvo', 'w_if', 'w_ift', 'b_if_row', 'b_if_col', 'mnorm', 'w_pa',
                              'w_u', 'conv_w', 'conv_b', 'ln_g', 'ln_b', 'w_pb', 'b_pb', 'w_g', 'b_merge',
                              'w_out', 'g_post')]
    state_shapes = [(batch, NUM_HEADS, dh, dh), (batch, NUM_HEADS, dh), (batch, NUM_HEADS, LANES),
                    (batch, conv_w - 1, d_b)]
    operands = [x] + (list(init) if has_init else []) + weights
    in_specs = ([x_spec] + ([_per_batch(s, nb) for s in state_shapes] if has_init else [])
                + [_resident(w.shape) for w in weights])
    out_shape = [jax.ShapeDtypeStruct(x.shape, F32)] + [jax.ShapeDtypeStruct(s, F32) for s in state_shapes]
    out_specs = [x_spec] + [_per_batch(s, nb) for s in state_shapes]
    rows = nb * tt
    return pl.pallas_call(
        functools.partial(_mixer_kernel, nb=nb, tt=tt, has_init=has_init),
        grid=(batch // nb, seq // tt),
        in_specs=in_specs, out_specs=out_specs, out_shape=out_shape,
        scratch_shapes=[pltpu.VMEM((rows, d_model), BF16), pltpu.VMEM((rows, d_a), BF16),
                        pltpu.VMEM((nb, CONV_HIST_ROWS + tt, d_b), F32), pltpu.VMEM((rows, d_b), F32)],
        compiler_params=pltpu.CompilerParams(dimension_semantics=("arbitrary", "arbitrary"),
                                             vmem_limit_bytes=VMEM_LIMIT_BYTES),
        name="mixer_init" if has_init else "mixer",
    )(*operands)


def _ffn_layer(x, init, p):
    batch, seq, d_model = x.shape
    nb, tt = _block_rows(batch, seq)
    d_ff = p['w_down'].shape[0]
    width = p['dw'].shape[0]
    has_init = init is not None
    x_spec = pl.BlockSpec((nb, tt, d_model), lambda b, t: (b, t, 0))
    weights = [p[k] for k in ('g_pre', 'w_up', 'dw', 'dw_b', 'w_down', 'g_post')]
    fb_shape = (batch, width - 1, 2 * d_ff)
    operands = [x] + ([init] if has_init else []) + weights
    in_specs = [x_spec] + ([_per_batch(fb_shape, nb)] if has_init else []) + [_resident(w.shape) for w in weights]
    return pl.pallas_call(
        functools.partial(_ffn_kernel, nb=nb, tt=tt, has_init=has_init),
        grid=(batch // nb, seq // tt),
        in_specs=in_specs,
        out_specs=[x_spec, _per_batch(fb_shape, nb)],
        out_shape=[jax.ShapeDtypeStruct(x.shape, F32), jax.ShapeDtypeStruct(fb_shape, F32)],
        scratch_shapes=[pltpu.VMEM((nb, FFN_HIST_ROWS + tt, 2 * d_ff), F32)],
        compiler_params=pltpu.CompilerParams(dimension_semantics=("arbitrary", "arbitrary"),
                                             vmem_limit_bytes=VMEM_LIMIT_BYTES),
        name="ffn_init" if has_init else "ffn",
    )(*operands)


def _row(v):
    return v.reshape(1, -1).astype(F32)


def _layer_params(l, norm_mix_pre, norm_mix_post, norm_ffn_pre, norm_ffn_post, w_in, b_i, b_f, mlstm_norm,
                  w_proj_a, conv_dw, conv_b, conv_ln_g, conv_ln_b, w_proj_b, b_proj_b, b_merge, w_out, w_up,
                  ffn_dw, ffn_dw_b, w_down):
    d_a = w_proj_a.shape[1]
    d_b = w_proj_b.shape[1]
    nh = b_i.shape[1]
    wl = w_in[l]
    o_if = 4 * d_a
    o_u = o_if + 2 * nh
    o_g = o_u + 2 * d_b
    w_if = wl[:, o_if:o_u]
    b_if = jnp.concatenate([b_i[l], b_f[l]]).astype(F32)
    mixer = {
        'g_pre': _row(norm_mix_pre[l]),
        'w_qkvo': wl[:, :o_if].astype(BF16),
        'w_if': jnp.pad(w_if, ((0, 0), (0, LANES - 2 * nh))).astype(BF16),
        'w_ift': w_if.T.astype(BF16),
        'b_if_row': jnp.pad(b_if, (0, LANES - 2 * nh)).reshape(1, LANES),
        'b_if_col': jnp.broadcast_to(b_if[:, None], (2 * nh, LANES)),
        'mnorm': _row(mlstm_norm[l]),
        'w_pa': w_proj_a[l].astype(BF16),
        'w_u': wl[:, o_u:o_g].astype(BF16),
        'conv_w': conv_dw[l].astype(F32),
        'conv_b': _row(conv_b[l]),
        'ln_g': _row(conv_ln_g[l]),
        'ln_b': _row(conv_ln_b[l]),
        'w_pb': w_proj_b[l].astype(BF16),
        'b_pb': _row(b_proj_b[l]),
        'w_g': wl[:, o_g:].astype(BF16),
        'b_merge': _row(b_merge[l]),
        'w_out': w_out[l].astype(BF16),
        'g_post': _row(norm_mix_post[l]),
    }
    ffn = {
        'g_pre': _row(norm_ffn_pre[l]),
        'w_up': w_up[l].astype(BF16),
        'dw': ffn_dw[l].astype(F32),
        'dw_b': _row(ffn_dw_b[l]),
        'w_down': w_down[l].astype(BF16),
        'g_post': _row(norm_ffn_post[l]),
    }
    return mixer, ffn


def kernel(x_prompt, x_sample, state_mlstm_C, state_mlstm_n, state_mlstm_m, cache_conv, cache_ffn_conv, norm_mix_pre, norm_mix_post, norm_ffn_pre, norm_ffn_post, w_in, b_i, b_f, mlstm_norm, w_proj_a, conv_dw, conv_b, conv_ln_g, conv_ln_b, w_proj_b, b_proj_b, b_merge, w_out, w_up, ffn_dw, ffn_dw_b, w_down):
    assert b_i.shape[1] == NUM_HEADS
    depth = w_in.shape[0]
    yp, ys = x_prompt, x_sample
    prompt_states, sample_states = [], []
    for l in range(depth):
        mixer_p, ffn_p = _layer_params(l, norm_mix_pre, norm_mix_post, norm_ffn_pre, norm_ffn_post, w_in, b_i,
                                       b_f, mlstm_norm, w_proj_a, conv_dw, conv_b, conv_ln_g, conv_ln_b,
                                       w_proj_b, b_proj_b, b_merge, w_out, w_up, ffn_dw, ffn_dw_b, w_down)
        yp, c1, n1, m1, cb1 = _mixer_layer(yp, None, mixer_p)
        yp, fb1 = _ffn_layer(yp, None, ffn_p)
        prompt_states.append((c1, n1, m1[..., 0], cb1, fb1))
        m0 = jnp.broadcast_to(state_mlstm_m[l][..., None], state_mlstm_m[l].shape + (LANES,))
        ys, c2, n2, m2, cb2 = _mixer_layer(ys, (state_mlstm_C[l], state_mlstm_n[l], m0, cache_conv[l]), mixer_p)
        ys, fb2 = _ffn_layer(ys, cache_ffn_conv[l], ffn_p)
        sample_states.append((c2, n2, m2[..., 0], cb2, fb2))
    stacked_p = [jnp.stack(s) for s in zip(*prompt_states)]
    stacked_s = [jnp.stack(s) for s in zip(*sample_states)]
    return (yp, ys, *stacked_p, *stacked_s)
```

```python
import functools
import math

import jax
import jax.numpy as jnp
from jax import lax
from jax.experimental import pallas as pl
from jax.experimental.pallas import tpu as pltpu

EPS = 1e-6
NUM_HEADS = 4
SUBLANES = 8
LANES = 128
CONV_HIST_ROWS = 32
FFN_HIST_ROWS = 8
MAX_BLOCK_ROWS = 256
CONV_PART_ROWS = 128
VMEM_LIMIT_BYTES = 56 * 1024 * 1024
STATE_VMEM_BYTES = 8 * 1024 * 1024

F32 = jnp.float32
BF16 = jnp.bfloat16


def _dot(a, b):
    return jnp.dot(a, b, preferred_element_type=F32)


def _wdot(a, w_ref, rows=slice(None), cols=slice(None)):
    return _dot(a, pltpu.bitcast(w_ref[rows, cols], BF16))


def _dot_nt(a, b):
    return lax.dot_general(a, b, (((1,), (1,)), ((), ())), preferred_element_type=F32)


def _dot_tn(a, b):
    return lax.dot_general(a, b, (((0,), (0,)), ((), ())), preferred_element_type=F32)


def _sigmoid(x):
    return 1.0 / (1.0 + jnp.exp(-x))


def _log_sigmoid(x):
    return jnp.minimum(x, 0.0) - jnp.log1p(jnp.exp(-jnp.abs(x)))


def _rms_scale(x):
    return x * lax.rsqrt(jnp.mean(x * x, axis=-1, keepdims=True) + EPS)


def _split_bf16(x):
    hi = x.astype(BF16)
    lo = (x - hi.astype(F32)).astype(BF16)
    return hi, lo


def _mlstm_gates(xb, w_if_ref, w_ift_ref, b_if_row_ref, b_if_col_ref, tril, triu):
    g_col = _dot(xb, w_if_ref[...]) + b_if_row_ref[...]
    hi, lo = _split_bf16(_log_sigmoid(g_col))
    bcum_col = _dot(tril, hi) + _dot(tril, lo)
    g_row = _dot_nt(w_ift_ref[...], xb) + b_if_col_ref[:, 0:1]
    hi, lo = _split_bf16(_log_sigmoid(g_row))
    bcum_row = _dot(hi, triu) + _dot(lo, triu)
    return g_col, bcum_col, g_row, bcum_row


def _mlstm_block(q, k, v, gates, h, causal, c_prev, n_prev, m_prev):
    tt = q.shape[0]
    g_col, bcum_col, g_row, bcum_row = gates
    ig_c = g_col[:, h:h + 1]
    bc = bcum_col[:, NUM_HEADS + h:NUM_HEADS + h + 1]
    ig_r = g_row[h:h + 1, :]
    br = bcum_row[NUM_HEADS + h:NUM_HEADS + h + 1, :]
    qb, kb, vb = q.astype(BF16), k.astype(BF16), v.astype(BF16)

    logd = jnp.where(causal, (bc - br) + ig_r, -jnp.inf)
    g = bc + m_prev
    m_tok = jnp.maximum(g, jnp.max(logd, axis=-1, keepdims=True))
    w = jnp.exp(logd - m_tok)
    inter = jnp.exp(g - m_tok)
    s = _dot_nt(qb, kb) * w
    num = _dot(s.astype(BF16), vb) + inter * _dot(qb, c_prev.astype(BF16))
    den = jnp.sum(s, axis=-1, keepdims=True) + inter * jnp.sum(q * n_prev, axis=-1, keepdims=True)
    hh = num * (1.0 / jnp.maximum(jnp.abs(den), jnp.exp(-m_tok)))

    m_new = m_tok[tt - 1:tt, :]
    decay = inter[tt - 1:tt, :]
    w_last = jnp.exp((bc[tt - 1:tt, :] - bc) + ig_c - m_new)
    c_new = decay * c_prev + _dot_tn(kb, (w_last * v).astype(BF16))
    n_new = decay * n_prev + jnp.sum(w_last * k, axis=0, keepdims=True)
    return hh, c_new, n_new, m_new


def _causal_conv_tile(glu_scr, conv_scr, conv_w_ref, conv_b_ref, b, tt, ls, hist0):
    conv_w = conv_w_ref.shape[0]
    part = min(tt, CONV_PART_ROWS)
    span = part + CONV_HIST_ROWS
    for p0 in range(0, tt, part):
        acc = jnp.broadcast_to(conv_b_ref[:, ls], (part, LANES))
        rows_in = glu_scr[b, p0:p0 + span, ls]
        for r in range(SUBLANES):
            offs = [o for o in range(hist0, hist0 + conv_w) if o % SUBLANES == r]
            window = rows_in if r == 0 else pltpu.roll(rows_in, span - r, axis=0)
            for o in offs:
                kk = o - hist0
                acc = acc + conv_w_ref[kk:kk + 1, ls] * window[o - r:o - r + part]
        conv_scr[b * tt + p0:b * tt + p0 + part, ls] = acc


def _mixer_kernel(*refs, nb, tt, has_init):
    if has_init:
        (x_ref, c0_ref, n0_ref, m0_ref, cb0_ref, *rest) = refs
    else:
        (x_ref, *rest) = refs
    (g_pre_ref, w_qkvo_ref, w_if_ref, w_ift_ref, b_if_row_ref, b_if_col_ref, mnorm_ref, w_pa_ref,
     w_u_ref, conv_w_ref, conv_b_ref, ln_g_ref, ln_b_ref, w_pb_ref, b_pb_ref, w_g_ref, b_merge_ref,
     w_out_ref, g_post_ref,
     y_ref, c_ref, n_ref, m_ref, cb_ref,
     xn_scr, hg_scr, glu_scr, conv_scr, gt_scr, ya_scr, gcol_scr, grow_scr) = rest

    t = pl.program_id(1)
    rows = nb * tt
    d_model = x_ref.shape[-1]
    d_a = mnorm_ref.shape[1]
    dh = d_a // NUM_HEADS
    d_b = conv_b_ref.shape[1]
    conv_w = conv_w_ref.shape[0]
    hist0 = CONV_HIST_ROWS - (conv_w - 1)

    @pl.when(t == 0)
    def _init_state():
        glu_scr[:, 0:CONV_HIST_ROWS, :] = jnp.zeros((nb, CONV_HIST_ROWS, d_b), F32)
        if has_init:
            c_ref[...] = c0_ref[...]
            n_ref[...] = n0_ref[...]
            m_ref[...] = m0_ref[...]
            glu_scr[:, hist0:CONV_HIST_ROWS, :] = cb0_ref[...]
        else:
            c_ref[...] = jnp.zeros(c_ref.shape, F32)
            n_ref[...] = jnp.zeros(n_ref.shape, F32)
            m_ref[...] = jnp.zeros(m_ref.shape, F32)

    x = x_ref[...].reshape(rows, d_model)
    xn_scr[...] = (_rms_scale(x) * g_pre_ref[...]).astype(BF16)

    row_id = lax.broadcasted_iota(jnp.int32, (tt, tt), 0)
    col_id = lax.broadcasted_iota(jnp.int32, (tt, tt), 1)
    tril = jnp.where(row_id >= col_id, 1.0, 0.0).astype(BF16)
    triu = jnp.where(col_id >= row_id, 1.0, 0.0).astype(BF16)
    for b in range(nb):
        g_col, bcum_col, g_row, bcum_row = _mlstm_gates(
            xn_scr[b * tt:(b + 1) * tt, :], w_if_ref, w_ift_ref, b_if_row_ref, b_if_col_ref, tril, triu)
        gcol_scr[b, 0], gcol_scr[b, 1] = g_col, bcum_col
        grow_scr[b, 0], grow_scr[b, 1] = g_row, bcum_row

    xn = xn_scr[...]
    u = _wdot(xn, w_u_ref)
    glu_scr[:, CONV_HIST_ROWS:CONV_HIST_ROWS + tt, :] = (u[:, :d_b] * _sigmoid(u[:, d_b:])).reshape(nb, tt, d_b)

    dg = d_model // NUM_HEADS
    causal = row_id >= col_id
    lane_tiles = d_b // LANES
    for h in range(NUM_HEADS):
        hs = slice(h * dh, (h + 1) * dh)
        q_all = _wdot(xn, w_qkvo_ref, cols=slice(h * dh, (h + 1) * dh))
        k_all = _wdot(xn, w_qkvo_ref, cols=slice(d_a + h * dh, d_a + (h + 1) * dh)) * (1.0 / math.sqrt(dh))
        v_all = _wdot(xn, w_qkvo_ref, cols=slice(2 * d_a + h * dh, 2 * d_a + (h + 1) * dh))
        o_all = _wdot(xn, w_qkvo_ref, cols=slice(3 * d_a + h * dh, 3 * d_a + (h + 1) * dh))
        for b in range(nb):
            rs = slice(b * tt, (b + 1) * tt)
            gates = (gcol_scr[b, 0], gcol_scr[b, 1], grow_scr[b, 0], grow_scr[b, 1])
            hh, c_new, n_new, m_new = _mlstm_block(
                q_all[rs], k_all[rs], v_all[rs], gates, h, causal,
                c_ref[b, h], n_ref[b, h:h + 1, :], m_ref[b, h:h + 1, 0:1])
            c_ref[b, h] = c_new
            n_ref[b, h:h + 1, :] = n_new
            m_ref[b, h:h + 1, :] = jnp.broadcast_to(m_new, (1, LANES))
            hn = _rms_scale(hh) * mnorm_ref[:, hs]
            hg_scr[rs, hs] = (hn * _sigmoid(o_all[rs])).astype(BF16)

        ya_h = _wdot(hg_scr[:, hs], w_pa_ref, rows=slice(h * dh // 2, (h + 1) * dh // 2))
        ya_scr[...] = ya_h if h == 0 else ya_scr[...] + ya_h
        for half in range(2):
            gs = slice(half * d_model + h * dg, half * d_model + (h + 1) * dg)
            gt_scr[:, gs] = _sigmoid(_wdot(xn, w_g_ref, cols=gs) + b_merge_ref[:, gs])
        for j in range(h * lane_tiles // NUM_HEADS, (h + 1) * lane_tiles // NUM_HEADS):
            for b in range(nb):
                _causal_conv_tile(glu_scr, conv_scr, conv_w_ref, conv_b_ref, b, tt,
                                  slice(j * LANES, (j + 1) * LANES), hist0)
    glu_scr[:, 0:CONV_HIST_ROWS, :] = glu_scr[:, tt:tt + CONV_HIST_ROWS, :]

    if nb % 2 == 0:
        splits = [(slice(i * nb // 2, (i + 1) * nb // 2), slice(0, tt)) for i in range(2)]
    elif nb == 1 and tt % (4 * SUBLANES) == 0:
        splits = [(slice(0, 1), slice(i * tt // 2, (i + 1) * tt // 2)) for i in range(2)]
    else:
        splits = [(slice(0, nb), slice(0, tt))]
    part_rows = rows // len(splits)
    for i, (bs, ts) in enumerate(splits):
        rs = slice(i * part_rows, (i + 1) * part_rows)
        c = conv_scr[rs, :]
        xc = c - jnp.mean(c, axis=-1, keepdims=True)
        cn = xc * lax.rsqrt(jnp.mean(xc * xc, axis=-1, keepdims=True) + EPS) * ln_g_ref[...] + ln_b_ref[...]
        act = cn * _sigmoid(cn)
        yb = _wdot(act.astype(BF16), w_pb_ref) + b_pb_ref[...]
        mix = gt_scr[rs, :d_model] * ya_scr[rs, :] + gt_scr[rs, d_model:] * yb
        mo = _wdot(mix.astype(BF16), w_out_ref)
        y = x_ref[bs, ts, :].reshape(part_rows, d_model) + _rms_scale(mo) * g_post_ref[...]
        y_ref[bs, ts, :] = y.reshape(bs.stop - bs.start, ts.stop - ts.start, d_model)

    @pl.when(t == pl.num_programs(1) - 1)
    def _emit_conv_state():
        cb_ref[...] = glu_scr[:, hist0:CONV_HIST_ROWS, :]


def _ffn_kernel(*refs, nb, tt, has_init):
    if has_init:
        (x_ref, fb0_ref, *rest) = refs
    else:
        (x_ref, *rest) = refs
    (g_pre_ref, w_up_ref, dw_ref, dw_b_ref, w_down_ref, g_post_ref,
     y_ref, fb_ref, hist_scr) = rest

    t = pl.program_id(1)
    rows = nb * tt
    d_model = x_ref.shape[-1]
    d_ff = dw_ref.shape[1] // 2
    width = dw_ref.shape[0]
    hist0 = FFN_HIST_ROWS - (width - 1)

    @pl.when(t == 0)
    def _init_state():
        hist_scr[...] = jnp.zeros(hist_scr.shape, F32)
        if has_init:
            hist_scr[:, hist0:FFN_HIST_ROWS, :] = fb0_ref[...]

    x = x_ref[...].reshape(rows, d_model)
    hn = (_rms_scale(x) * g_pre_ref[...]).astype(BF16)
    up = _wdot(hn, w_up_ref)

    sub_row = lax.broadcasted_iota(jnp.int32, (SUBLANES, 2 * d_ff), 0)
    conv = dw_ref[width - 1:width, :] * up + dw_b_ref[...]
    for kk in range(width - 1):
        shift = width - 1 - kk
        moved = pltpu.roll(up, shift, axis=0)
        pieces = []
        for b in range(nb):
            head = jnp.where(sub_row < shift, pltpu.roll(hist_scr[b], shift, axis=0),
                             moved[b * tt:b * tt + SUBLANES])
            pieces += [head, moved[b * tt + SUBLANES:(b + 1) * tt]]
        conv = conv + dw_ref[kk:kk + 1, :] * jnp.concatenate(pieces, axis=0)
    for b in range(nb):
        hist_scr[b] = up[(b + 1) * tt - FFN_HIST_ROWS:(b + 1) * tt]

    gate = conv[:, :d_ff]
    cdf = 0.5 * (1.0 + jnp.tanh(math.sqrt(2.0 / math.pi) * (gate + 0.044715 * (gate * gate * gate))))
    f = _wdot((gate * cdf * conv[:, d_ff:]).astype(BF16), w_down_ref)
    y = x_ref[...].reshape(rows, d_model) + _rms_scale(f) * g_post_ref[...]
    y_ref[...] = y.reshape(nb, tt, d_model)

    @pl.when(t == pl.num_programs(1) - 1)
    def _emit_conv_state():
        fb_ref[...] = hist_scr[:, hist0:FFN_HIST_ROWS, :]


def _block_rows(batch, seq, bytes_per_batch_row=0):
    tt = MAX_BLOCK_ROWS if seq % MAX_BLOCK_ROWS == 0 else seq
    assert tt % (2 * SUBLANES) == 0 and tt >= CONV_HIST_ROWS, (batch, seq)
    nb = max(1, min(batch, MAX_BLOCK_ROWS // tt))
    if bytes_per_batch_row:
        nb = max(1, min(nb, STATE_VMEM_BYTES // bytes_per_batch_row))
    while batch % nb:
        nb -= 1
    return nb, tt


def _resident(shape):
    return pl.BlockSpec(shape, lambda b, t: (0,) * len(shape), pipeline_mode=pl.Buffered(1))


def _per_batch(shape, nb):
    return pl.BlockSpec((nb,) + tuple(shape[1:]), lambda b, t: (b,) + (0,) * (len(shape) - 1))


def _mixer_layer(x, init, p):
    batch, seq, d_model = x.shape
    d_a = p['mnorm'].shape[1]
    dh = d_a // NUM_HEADS
    d_b = p['conv_b'].shape[1]
    conv_w = p['conv_w'].shape[0]
    has_init = init is not None
    memory_bytes = NUM_HEADS * dh * dh * 4 * (4 if has_init else 2)
    nb, tt = _block_rows(batch, seq, memory_bytes)
    x_spec = pl.BlockSpec((nb, tt, d_model), lambda b, t: (b, t, 0))
    weights = [p[k] for k in ('g_pre', 'w_qkvo', 'w_if', 'w_ift', 'b_if_row', 'b_if_col', 'mnorm', 'w_pa',
                              'w_u', 'conv_w', 'conv_b', 'ln_g', 'ln_b', 'w_pb', 'b_pb', 'w_g', 'b_merge',
                              'w_out', 'g_post')]
    state_shapes = [(batch, NUM_HEADS, dh, dh), (batch, NUM_HEADS, dh), (batch, NUM_HEADS, LANES),
                    (batch, conv_w - 1, d_b)]
    operands = [x] + (list(init) if has_init else []) + weights
    in_specs = ([x_spec] + ([_per_batch(s, nb) for s in state_shapes] if has_init else [])
                + [_resident(w.shape) for w in weights])
    out_shape = [jax.ShapeDtypeStruct(x.shape, F32)] + [jax.ShapeDtypeStruct(s, F32) for s in state_shapes]
    out_specs = [x_spec] + [_per_batch(s, nb) for s in state_shapes]
    rows = nb * tt
    return pl.pallas_call(
        functools.partial(_mixer_kernel, nb=nb, tt=tt, has_init=has_init),
        grid=(batch // nb, seq // tt),
        in_specs=in_specs, out_specs=out_specs, out_shape=out_shape,
        scratch_shapes=[pltpu.VMEM((rows, d_model), BF16), pltpu.VMEM((rows, d_a), BF16),
                        pltpu.VMEM((nb, CONV_HIST_ROWS + tt, d_b), F32),
                        pltpu.VMEM((rows, d_b), F32), pltpu.VMEM((rows, 2 * d_model), F32),
                        pltpu.VMEM((rows, d_model), F32), pltpu.VMEM((nb, 2, tt, LANES), F32),
                        pltpu.VMEM((nb, 2, SUBLANES, tt), F32)],
        compiler_params=pltpu.CompilerParams(dimension_semantics=("arbitrary", "arbitrary"),
                                             vmem_limit_bytes=VMEM_LIMIT_BYTES),
        name="mixer_init" if has_init else "mixer",
    )(*operands)


def _ffn_layer(x, init, p):
    batch, seq, d_model = x.shape
    nb, tt = _block_rows(batch, seq)
    d_ff = p['dw'].shape[1] // 2
    width = p['dw'].shape[0]
    has_init = init is not None
    x_spec = pl.BlockSpec((nb, tt, d_model), lambda b, t: (b, t, 0))
    weights = [p[k] for k in ('g_pre', 'w_up', 'dw', 'dw_b', 'w_down', 'g_post')]
    fb_shape = (batch, width - 1, 2 * d_ff)
    operands = [x] + ([init] if has_init else []) + weights
    in_specs = [x_spec] + ([_per_batch(fb_shape, nb)] if has_init else []) + [_resident(w.shape) for w in weights]
    return pl.pallas_call(
        functools.partial(_ffn_kernel, nb=nb, tt=tt, has_init=has_init),
        grid=(batch // nb, seq // tt),
        in_specs=in_specs,
        out_specs=[x_spec, _per_batch(fb_shape, nb)],
        out_shape=[jax.ShapeDtypeStruct(x.shape, F32), jax.ShapeDtypeStruct(fb_shape, F32)],
        scratch_shapes=[pltpu.VMEM((nb, FFN_HIST_ROWS, 2 * d_ff), F32)],
        compiler_params=pltpu.CompilerParams(dimension_semantics=("arbitrary", "arbitrary"),
                                             vmem_limit_bytes=VMEM_LIMIT_BYTES),
        name="ffn_init" if has_init else "ffn",
    )(*operands)


def _row(v):
    return v.reshape(1, -1).astype(F32)


def _pack_rows(w):
    bits = lax.bitcast_convert_type(w.astype(BF16), jnp.uint16).astype(jnp.uint32)
    return bits[0::2] | (bits[1::2] << 16)


def _layer_params(l, norm_mix_pre, norm_mix_post, norm_ffn_pre, norm_ffn_post, w_in, b_i, b_f, mlstm_norm,
                  w_proj_a, conv_dw, conv_b, conv_ln_g, conv_ln_b, w_proj_b, b_proj_b, b_merge, w_out, w_up,
                  ffn_dw, ffn_dw_b, w_down):
    d_a = w_proj_a.shape[1]
    d_b = w_proj_b.shape[1]
    nh = b_i.shape[1]
    wl = w_in[l]
    o_if = 4 * d_a
    o_u = o_if + 2 * nh
    o_g = o_u + 2 * d_b
    w_if = wl[:, o_if:o_u]
    b_if = jnp.concatenate([b_i[l], b_f[l]]).astype(F32)
    mixer = {
        'g_pre': _row(norm_mix_pre[l]),
        'w_qkvo': _pack_rows(wl[:, :o_if]),
        'w_if': jnp.pad(w_if, ((0, 0), (0, LANES - 2 * nh))).astype(BF16),
        'w_ift': w_if.T.astype(BF16),
        'b_if_row': jnp.pad(b_if, (0, LANES - 2 * nh)).reshape(1, LANES),
        'b_if_col': jnp.broadcast_to(b_if[:, None], (2 * nh, LANES)),
        'mnorm': _row(mlstm_norm[l]),
        'w_pa': _pack_rows(w_proj_a[l]),
        'w_u': _pack_rows(wl[:, o_u:o_g]),
        'conv_w': conv_dw[l].astype(F32),
        'conv_b': _row(conv_b[l]),
        'ln_g': _row(conv_ln_g[l]),
        'ln_b': _row(conv_ln_b[l]),
        'w_pb': _pack_rows(w_proj_b[l]),
        'b_pb': _row(b_proj_b[l]),
        'w_g': _pack_rows(wl[:, o_g:]),
        'b_merge': _row(b_merge[l]),
        'w_out': _pack_rows(w_out[l]),
        'g_post': _row(norm_mix_post[l]),
    }
    ffn = {
        'g_pre': _row(norm_ffn_pre[l]),
        'w_up': _pack_rows(w_up[l]),
        'dw': ffn_dw[l].astype(F32),
        'dw_b': _row(ffn_dw_b[l]),
        'w_down': _pack_rows(w_down[l]),
        'g_post': _row(norm_ffn_post[l]),
    }
    return mixer, ffn


def kernel(x_prompt, x_sample, state_mlstm_C, state_mlstm_n, state_mlstm_m, cache_conv, cache_ffn_conv, norm_mix_pre, norm_mix_post, norm_ffn_pre, norm_ffn_post, w_in, b_i, b_f, mlstm_norm, w_proj_a, conv_dw, conv_b, conv_ln_g, conv_ln_b, w_proj_b, b_proj_b, b_merge, w_out, w_up, ffn_dw, ffn_dw_b, w_down):
    assert b_i.shape[1] == NUM_HEADS
    depth = w_in.shape[0]
    yp, ys = x_prompt, x_sample
    prompt_states, sample_states = [], []
    for l in range(depth):
        mixer_p, ffn_p = _layer_params(l, norm_mix_pre, norm_mix_post, norm_ffn_pre, norm_ffn_post, w_in, b_i,
                                       b_f, mlstm_norm, w_proj_a, conv_dw, conv_b, conv_ln_g, conv_ln_b,
                                       w_proj_b, b_proj_b, b_merge, w_out, w_up, ffn_dw, ffn_dw_b, w_down)
        yp, c1, n1, m1, cb1 = _mixer_layer(yp, None, mixer_p)
        yp, fb1 = _ffn_layer(yp, None, ffn_p)
        prompt_states.append((c1, n1, m1[..., 0], cb1, fb1))
        m0 = jnp.broadcast_to(state_mlstm_m[l][..., None], state_mlstm_m[l].shape + (LANES,))
        ys, c2, n2, m2, cb2 = _mixer_layer(ys, (state_mlstm_C[l], state_mlstm_n[l], m0, cache_conv[l]), mixer_p)
        ys, fb2 = _ffn_layer(ys, cache_ffn_conv[l], ffn_p)
        sample_states.append((c2, n2, m2[..., 0], cb2, fb2))
    stacked_p = [jnp.stack(s) for s in zip(*prompt_states)]
    stacked_s = [jnp.stack(s) for s in zip(*sample_states)]
    return (yp, ys, *stacked_p, *stacked_s)
```

```python
import functools
import math

import jax
import jax.numpy as jnp
from jax import lax
from jax.experimental import pallas as pl
from jax.experimental.pallas import tpu as pltpu

EPS = 1e-6
NUM_HEADS = 4
SUBLANES = 8
LANES = 128
CONV_HIST_ROWS = 32
FFN_HIST_ROWS = 8
BLOCK_POSITIONS = 256
MIXER_BLOCK_ROWS = 512
FFN_BLOCK_ROWS = 256
CONV_PART_ROWS = 128
VMEM_LIMIT_BYTES = 56 * 1024 * 1024
STATE_VMEM_BYTES = 8 * 1024 * 1024
PACK_BLOCK_BYTES = 6 * 1024 * 1024

F32 = jnp.float32
BF16 = jnp.bfloat16


def _dot(a, b):
    return jnp.dot(a, b, preferred_element_type=F32)


def _wdot(a, w_ref, rows=slice(None), cols=slice(None)):
    return _dot(a, pltpu.bitcast(w_ref[rows, cols], BF16))


def _dot_nt(a, b):
    return lax.dot_general(a, b, (((1,), (1,)), ((), ())), preferred_element_type=F32)


def _dot_tn(a, b):
    return lax.dot_general(a, b, (((0,), (0,)), ((), ())), preferred_element_type=F32)


def _sigmoid(x):
    return 1.0 / (1.0 + jnp.exp(-x))


def _log_sigmoid(x):
    return jnp.minimum(x, 0.0) - jnp.log1p(jnp.exp(-jnp.abs(x)))


def _rms_scale(x):
    return x * lax.rsqrt(jnp.mean(x * x, axis=-1, keepdims=True) + EPS)


def _split_bf16(x):
    hi = x.astype(BF16)
    lo = (x - hi.astype(F32)).astype(BF16)
    return hi, lo


def _mlstm_gates(xb, w_if_ref, w_ift_ref, b_if_row_ref, b_if_col_ref, tril, triu):
    g_col = _dot(xb, w_if_ref[...]) + b_if_row_ref[...]
    hi, lo = _split_bf16(_log_sigmoid(g_col))
    bcum_col = _dot(tril, hi) + _dot(tril, lo)
    g_row = _dot_nt(w_ift_ref[...], xb) + b_if_col_ref[:, 0:1]
    hi, lo = _split_bf16(_log_sigmoid(g_row))
    bcum_row = _dot(hi, triu) + _dot(lo, triu)
    return g_col, bcum_col, g_row, bcum_row


def _mlstm_block(q, k, v, gates, h, causal, c_prev, n_prev, m_prev):
    tt = q.shape[0]
    g_col, bcum_col, g_row, bcum_row = gates
    ig_c = g_col[:, h:h + 1]
    bc = bcum_col[:, NUM_HEADS + h:NUM_HEADS + h + 1]
    ig_r = g_row[h:h + 1, :]
    br = bcum_row[NUM_HEADS + h:NUM_HEADS + h + 1, :]
    qb, kb, vb = q.astype(BF16), k.astype(BF16), v.astype(BF16)

    logd = jnp.where(causal, (bc - br) + ig_r, -jnp.inf)
    g = bc + m_prev
    m_tok = jnp.maximum(g, jnp.max(logd, axis=-1, keepdims=True))
    w = jnp.exp(logd - m_tok)
    inter = jnp.exp(g - m_tok)
    s = _dot_nt(qb, kb) * w
    num = _dot(s.astype(BF16), vb) + inter * _dot(qb, c_prev.astype(BF16))
    den = jnp.sum(s, axis=-1, keepdims=True) + inter * jnp.sum(q * n_prev, axis=-1, keepdims=True)
    hh = num * (1.0 / jnp.maximum(jnp.abs(den), jnp.exp(-m_tok)))

    m_new = m_tok[tt - 1:tt, :]
    decay = inter[tt - 1:tt, :]
    w_last = jnp.exp((bc[tt - 1:tt, :] - bc) + ig_c - m_new)
    c_new = decay * c_prev + _dot_tn(kb, (w_last * v).astype(BF16))
    n_new = decay * n_prev + jnp.sum(w_last * k, axis=0, keepdims=True)
    return hh, c_new, n_new, m_new


def _causal_conv_tile(glu_scr, conv_scr, conv_w_ref, conv_b_ref, b, tt, ls, hist0):
    conv_w = conv_w_ref.shape[0]
    part = min(tt, CONV_PART_ROWS)
    span = part + CONV_HIST_ROWS
    for p0 in range(0, tt, part):
        acc = jnp.broadcast_to(conv_b_ref[:, ls], (part, LANES))
        rows_in = glu_scr[b, p0:p0 + span, ls]
        for r in range(SUBLANES):
            offs = [o for o in range(hist0, hist0 + conv_w) if o % SUBLANES == r]
            window = rows_in if r == 0 else pltpu.roll(rows_in, span - r, axis=0)
            for o in offs:
                kk = o - hist0
                acc = acc + conv_w_ref[kk:kk + 1, ls] * window[o - r:o - r + part]
        conv_scr[b * tt + p0:b * tt + p0 + part, ls] = acc


def _mixer_kernel(*refs, nb, tt, has_init):
    if has_init:
        (x_ref, c0_ref, n0_ref, m0_ref, cb0_ref, *rest) = refs
    else:
        (x_ref, *rest) = refs
    (g_pre_ref, w_qkvo_ref, w_if_ref, w_ift_ref, b_if_row_ref, b_if_col_ref, mnorm_ref, w_pa_ref,
     w_u_ref, conv_w_ref, conv_b_ref, ln_g_ref, ln_b_ref, w_pb_ref, b_pb_ref, w_g_ref, b_merge_ref,
     w_out_ref, g_post_ref,
     y_ref, c_ref, n_ref, m_ref, cb_ref,
     xn_scr, hg_scr, glu_scr, conv_scr, gt_scr, ya_scr, gcol_scr, grow_scr) = rest

    t = pl.program_id(1)
    rows = nb * tt
    d_model = x_ref.shape[-1]
    d_a = mnorm_ref.shape[1]
    dh = d_a // NUM_HEADS
    d_b = conv_b_ref.shape[1]
    conv_w = conv_w_ref.shape[0]
    hist0 = CONV_HIST_ROWS - (conv_w - 1)

    @pl.when(t == 0)
    def _init_state():
        glu_scr[:, 0:CONV_HIST_ROWS, :] = jnp.zeros((nb, CONV_HIST_ROWS, d_b), F32)
        if has_init:
            c_ref[...] = c0_ref[...]
            n_ref[...] = n0_ref[...]
            m_ref[...] = m0_ref[...]
            glu_scr[:, hist0:CONV_HIST_ROWS, :] = cb0_ref[...]
        else:
            c_ref[...] = jnp.zeros(c_ref.shape, F32)
            n_ref[...] = jnp.zeros(n_ref.shape, F32)
            m_ref[...] = jnp.zeros(m_ref.shape, F32)

    x = x_ref[...].reshape(rows, d_model)
    xn_scr[...] = (_rms_scale(x) * g_pre_ref[...]).astype(BF16)

    row_id = lax.broadcasted_iota(jnp.int32, (tt, tt), 0)
    col_id = lax.broadcasted_iota(jnp.int32, (tt, tt), 1)
    tril = jnp.where(row_id >= col_id, 1.0, 0.0).astype(BF16)
    triu = jnp.where(col_id >= row_id, 1.0, 0.0).astype(BF16)
    for b in range(nb):
        g_col, bcum_col, g_row, bcum_row = _mlstm_gates(
            xn_scr[b * tt:(b + 1) * tt, :], w_if_ref, w_ift_ref, b_if_row_ref, b_if_col_ref, tril, triu)
        gcol_scr[b, 0], gcol_scr[b, 1] = g_col, bcum_col
        grow_scr[b, 0], grow_scr[b, 1] = g_row, bcum_row

    xn = xn_scr[...]
    u = _wdot(xn, w_u_ref)
    glu_scr[:, CONV_HIST_ROWS:CONV_HIST_ROWS + tt, :] = (u[:, :d_b] * _sigmoid(u[:, d_b:])).reshape(nb, tt, d_b)

    dg = d_model // NUM_HEADS
    causal = row_id >= col_id
    lane_tiles = d_b // LANES
    for h in range(NUM_HEADS):
        hs = slice(h * dh, (h + 1) * dh)
        q_all = _wdot(xn, w_qkvo_ref, cols=slice(h * dh, (h + 1) * dh))
        k_all = _wdot(xn, w_qkvo_ref, cols=slice(d_a + h * dh, d_a + (h + 1) * dh)) * (1.0 / math.sqrt(dh))
        v_all = _wdot(xn, w_qkvo_ref, cols=slice(2 * d_a + h * dh, 2 * d_a + (h + 1) * dh))
        o_all = _wdot(xn, w_qkvo_ref, cols=slice(3 * d_a + h * dh, 3 * d_a + (h + 1) * dh))
        for b in range(nb):
            rs = slice(b * tt, (b + 1) * tt)
            gates = (gcol_scr[b, 0], gcol_scr[b, 1], grow_scr[b, 0], grow_scr[b, 1])
            hh, c_new, n_new, m_new = _mlstm_block(
                q_all[rs], k_all[rs], v_all[rs], gates, h, causal,
                c_ref[b, h], n_ref[b, h:h + 1, :], m_ref[b, h:h + 1, 0:1])
            c_ref[b, h] = c_new
            n_ref[b, h:h + 1, :] = n_new
            m_ref[b, h:h + 1, :] = jnp.broadcast_to(m_new, (1, LANES))
            hn = _rms_scale(hh) * mnorm_ref[:, hs]
            hg_scr[rs, hs] = (hn * _sigmoid(o_all[rs])).astype(BF16)

        ya_h = _wdot(hg_scr[:, hs], w_pa_ref, rows=slice(h * dh // 2, (h + 1) * dh // 2))
        ya_scr[...] = ya_h if h == 0 else ya_scr[...] + ya_h
        for half in range(2):
            gs = slice(half * d_model + h * dg, half * d_model + (h + 1) * dg)
            gt_scr[:, gs] = _sigmoid(_wdot(xn, w_g_ref, cols=gs) + b_merge_ref[:, gs])
        for j in range(h * lane_tiles // NUM_HEADS, (h + 1) * lane_tiles // NUM_HEADS):
            for b in range(nb):
                _causal_conv_tile(glu_scr, conv_scr, conv_w_ref, conv_b_ref, b, tt,
                                  slice(j * LANES, (j + 1) * LANES), hist0)
    glu_scr[:, 0:CONV_HIST_ROWS, :] = glu_scr[:, tt:tt + CONV_HIST_ROWS, :]

    if nb % 2 == 0:
        splits = [(slice(i * nb // 2, (i + 1) * nb // 2), slice(0, tt)) for i in range(2)]
    elif nb == 1 and tt % (4 * SUBLANES) == 0:
        splits = [(slice(0, 1), slice(i * tt // 2, (i + 1) * tt // 2)) for i in range(2)]
    else:
        splits = [(slice(0, nb), slice(0, tt))]
    part_rows = rows // len(splits)
    for i, (bs, ts) in enumerate(splits):
        rs = slice(i * part_rows, (i + 1) * part_rows)
        c = conv_scr[rs, :]
        xc = c - jnp.mean(c, axis=-1, keepdims=True)
        cn = xc * lax.rsqrt(jnp.mean(xc * xc, axis=-1, keepdims=True) + EPS) * ln_g_ref[...] + ln_b_ref[...]
        act = cn * _sigmoid(cn)
        yb = _wdot(act.astype(BF16), w_pb_ref) + b_pb_ref[...]
        mix = gt_scr[rs, :d_model] * ya_scr[rs, :] + gt_scr[rs, d_model:] * yb
        mo = _wdot(mix.astype(BF16), w_out_ref)
        y = x_ref[bs, ts, :].reshape(part_rows, d_model) + _rms_scale(mo) * g_post_ref[...]
        y_ref[bs, ts, :] = y.reshape(bs.stop - bs.start, ts.stop - ts.start, d_model)

    @pl.when(t == pl.num_programs(1) - 1)
    def _emit_conv_state():
        cb_ref[...] = glu_scr[:, hist0:CONV_HIST_ROWS, :]


def _ffn_kernel(*refs, nb, tt, has_init):
    if has_init:
        (x_ref, fb0_ref, *rest) = refs
    else:
        (x_ref, *rest) = refs
    (g_pre_ref, w_up_ref, dw_ref, dw_b_ref, w_down_ref, g_post_ref,
     y_ref, fb_ref, hist_scr) = rest

    t = pl.program_id(1)
    rows = nb * tt
    d_model = x_ref.shape[-1]
    d_ff = dw_ref.shape[1] // 2
    width = dw_ref.shape[0]
    hist0 = FFN_HIST_ROWS - (width - 1)

    @pl.when(t == 0)
    def _init_state():
        hist_scr[...] = jnp.zeros(hist_scr.shape, F32)
        if has_init:
            hist_scr[:, hist0:FFN_HIST_ROWS, :] = fb0_ref[...]

    x = x_ref[...].reshape(rows, d_model)
    hn = (_rms_scale(x) * g_pre_ref[...]).astype(BF16)
    up = _wdot(hn, w_up_ref)

    sub_row = lax.broadcasted_iota(jnp.int32, (SUBLANES, 2 * d_ff), 0)
    conv = dw_ref[width - 1:width, :] * up + dw_b_ref[...]
    for kk in range(width - 1):
        shift = width - 1 - kk
        moved = pltpu.roll(up, shift, axis=0)
        pieces = []
        for b in range(nb):
            head = jnp.where(sub_row < shift, pltpu.roll(hist_scr[b], shift, axis=0),
                             moved[b * tt:b * tt + SUBLANES])
            pieces += [head, moved[b * tt + SUBLANES:(b + 1) * tt]]
        conv = conv + dw_ref[kk:kk + 1, :] * jnp.concatenate(pieces, axis=0)
    for b in range(nb):
        hist_scr[b] = up[(b + 1) * tt - FFN_HIST_ROWS:(b + 1) * tt]

    gate = conv[:, :d_ff]
    cdf = 0.5 * (1.0 + jnp.tanh(math.sqrt(2.0 / math.pi) * (gate + 0.044715 * (gate * gate * gate))))
    f = _wdot((gate * cdf * conv[:, d_ff:]).astype(BF16), w_down_ref)
    y = x_ref[...].reshape(rows, d_model) + _rms_scale(f) * g_post_ref[...]
    y_ref[...] = y.reshape(nb, tt, d_model)

    @pl.when(t == pl.num_programs(1) - 1)
    def _emit_conv_state():
        fb_ref[...] = hist_scr[:, hist0:FFN_HIST_ROWS, :]


def _block_rows(batch, seq, max_rows, bytes_per_batch_row=0):
    tt = BLOCK_POSITIONS if seq % BLOCK_POSITIONS == 0 else seq
    assert tt % (2 * SUBLANES) == 0 and tt >= CONV_HIST_ROWS, (batch, seq)
    nb = max(1, min(batch, max_rows // tt))
    if bytes_per_batch_row:
        nb = max(1, min(nb, STATE_VMEM_BYTES // bytes_per_batch_row))
    while batch % nb:
        nb -= 1
    return nb, tt


def _layer_spec(stacked, layer):
    zeros = (0,) * (stacked.ndim - 1)
    return pl.BlockSpec((None,) + stacked.shape[1:], lambda b, t: (layer,) + zeros, pipeline_mode=pl.Buffered(1))


def _state_spec(stacked_shape, layer, nb):
    zeros = (0,) * (len(stacked_shape) - 2)
    return pl.BlockSpec((None, nb) + tuple(stacked_shape[2:]), lambda b, t: (layer, b) + zeros)


def _per_batch(shape, nb):
    return pl.BlockSpec((nb,) + tuple(shape[1:]), lambda b, t: (b,) + (0,) * (len(shape) - 1))


MIXER_PARAMS = ('g_pre', 'w_qkvo', 'w_if', 'w_ift', 'b_if_row', 'b_if_col', 'mnorm', 'w_pa', 'w_u', 'conv_w',
                'conv_b', 'ln_g', 'ln_b', 'w_pb', 'b_pb', 'w_g', 'b_merge', 'w_out', 'g_post')
FFN_PARAMS = ('g_pre', 'w_up', 'dw', 'dw_b', 'w_down', 'g_post')


def _mixer_layer(x, init, p, layer):
    batch, seq, d_model = x.shape
    d_a = p['mnorm'].shape[-1]
    dh = d_a // NUM_HEADS
    d_b = p['conv_b'].shape[-1]
    conv_w = p['conv_w'].shape[-2]
    has_init = init is not None
    memory_bytes = NUM_HEADS * dh * dh * 4 * (4 if has_init else 2)
    nb, tt = _block_rows(batch, seq, MIXER_BLOCK_ROWS, memory_bytes)
    x_spec = pl.BlockSpec((nb, tt, d_model), lambda b, t: (b, t, 0))
    weights = [p[k] for k in MIXER_PARAMS]
    state_shapes = [(batch, NUM_HEADS, dh, dh), (batch, NUM_HEADS, dh), (batch, NUM_HEADS, LANES),
                    (batch, conv_w - 1, d_b)]
    operands = [x] + (list(init) if has_init else []) + weights
    in_specs = ([x_spec] + ([_state_spec(s.shape, layer, nb) for s in init] if has_init else [])
                + [_layer_spec(w, layer) for w in weights])
    out_shape = [jax.ShapeDtypeStruct(x.shape, F32)] + [jax.ShapeDtypeStruct(s, F32) for s in state_shapes]
    out_specs = [x_spec] + [_per_batch(s, nb) for s in state_shapes]
    rows = nb * tt
    return pl.pallas_call(
        functools.partial(_mixer_kernel, nb=nb, tt=tt, has_init=has_init),
        grid=(batch // nb, seq // tt),
        in_specs=in_specs, out_specs=out_specs, out_shape=out_shape,
        scratch_shapes=[pltpu.VMEM((rows, d_model), BF16), pltpu.VMEM((rows, d_a), BF16),
                        pltpu.VMEM((nb, CONV_HIST_ROWS + tt, d_b), F32),
                        pltpu.VMEM((rows, d_b), F32), pltpu.VMEM((rows, 2 * d_model), F32),
                        pltpu.VMEM((rows, d_model), F32), pltpu.VMEM((nb, 2, tt, LANES), F32),
                        pltpu.VMEM((nb, 2, SUBLANES, tt), F32)],
        compiler_params=pltpu.CompilerParams(dimension_semantics=("arbitrary", "arbitrary"),
                                             vmem_limit_bytes=VMEM_LIMIT_BYTES),
        name="mixer_init" if has_init else "mixer",
    )(*operands)


def _ffn_layer(x, init, p, layer):
    batch, seq, d_model = x.shape
    nb, tt = _block_rows(batch, seq, FFN_BLOCK_ROWS)
    d_ff = p['dw'].shape[-1] // 2
    width = p['dw'].shape[-2]
    has_init = init is not None
    x_spec = pl.BlockSpec((nb, tt, d_model), lambda b, t: (b, t, 0))
    weights = [p[k] for k in FFN_PARAMS]
    fb_shape = (batch, width - 1, 2 * d_ff)
    operands = [x] + ([init] if has_init else []) + weights
    in_specs = ([x_spec] + ([_state_spec(init.shape, layer, nb)] if has_init else [])
                + [_layer_spec(w, layer) for w in weights])
    return pl.pallas_call(
        functools.partial(_ffn_kernel, nb=nb, tt=tt, has_init=has_init),
        grid=(batch // nb, seq // tt),
        in_specs=in_specs,
        out_specs=[x_spec, _per_batch(fb_shape, nb)],
        out_shape=[jax.ShapeDtypeStruct(x.shape, F32), jax.ShapeDtypeStruct(fb_shape, F32)],
        scratch_shapes=[pltpu.VMEM((nb, FFN_HIST_ROWS, 2 * d_ff), F32)],
        compiler_params=pltpu.CompilerParams(dimension_semantics=("arbitrary", "arbitrary"),
                                             vmem_limit_bytes=VMEM_LIMIT_BYTES),
        name="ffn_init" if has_init else "ffn",
    )(*operands)


def _pack_kernel(w_ref, o_ref):
    o_ref[...] = pltpu.bitcast(w_ref[...].astype(BF16), jnp.uint32)


def _pack_weights(w, cols=None):
    depth, k, n = w.shape
    n = n if cols is None else cols
    tn = max(c for c in range(LANES, n + 1, LANES) if n % c == 0 and k * c * w.dtype.itemsize <= PACK_BLOCK_BYTES)
    return pl.pallas_call(
        _pack_kernel,
        grid=(depth, n // tn),
        in_specs=[pl.BlockSpec((None, k, tn), lambda l, j: (l, 0, j))],
        out_specs=pl.BlockSpec((None, k // 2, tn), lambda l, j: (l, 0, j)),
        out_shape=jax.ShapeDtypeStruct((depth, k // 2, n), jnp.uint32),
        name="pack_weights",
    )(w)


def _rows(v):
    return v.reshape(v.shape[0], 1, v.shape[1]).astype(F32)


def _param_stacks(norm_mix_pre, norm_mix_post, norm_ffn_pre, norm_ffn_post, w_in, b_i, b_f, mlstm_norm, w_proj_a,
                  conv_dw, conv_b, conv_ln_g, conv_ln_b, w_proj_b, b_proj_b, b_merge, w_out, w_up, ffn_dw,
                  ffn_dw_b, w_down):
    depth = w_in.shape[0]
    d_a = w_proj_a.shape[1]
    d_b = w_proj_b.shape[1]
    nh = b_i.shape[1]
    o_if = 4 * d_a
    o_u = o_if + 2 * nh
    o_g = o_u + 2 * d_b
    w_if = w_in[:, :, o_if:o_u]
    b_if = jnp.concatenate([b_i, b_f], axis=1).astype(F32)
    mixer = {
        'g_pre': _rows(norm_mix_pre),
        'w_qkvo': _pack_weights(w_in, cols=o_if),
        'w_if': jnp.pad(w_if, ((0, 0), (0, 0), (0, LANES - 2 * nh))).astype(BF16),
        'w_ift': jnp.swapaxes(w_if, 1, 2).astype(BF16),
        'b_if_row': jnp.pad(b_if, ((0, 0), (0, LANES - 2 * nh))).reshape(depth, 1, LANES),
        'b_if_col': jnp.broadcast_to(b_if[:, :, None], (depth, 2 * nh, LANES)),
        'mnorm': _rows(mlstm_norm),
        'w_pa': _pack_weights(w_proj_a),
        'w_u': _pack_weights(w_in[:, :, o_u:o_g].astype(BF16)),
        'conv_w': conv_dw.astype(F32),
        'conv_b': _rows(conv_b),
        'ln_g': _rows(conv_ln_g),
        'ln_b': _rows(conv_ln_b),
        'w_pb': _pack_weights(w_proj_b),
        'b_pb': _rows(b_proj_b),
        'w_g': _pack_weights(w_in[:, :, o_g:].astype(BF16)),
        'b_merge': _rows(b_merge),
        'w_out': _pack_weights(w_out),
        'g_post': _rows(norm_mix_post),
    }
    ffn = {
        'g_pre': _rows(norm_ffn_pre),
        'w_up': _pack_weights(w_up),
        'dw': ffn_dw.astype(F32),
        'dw_b': _rows(ffn_dw_b),
        'w_down': _pack_weights(w_down),
        'g_post': _rows(norm_ffn_post),
    }
    return mixer, ffn


def kernel(x_prompt, x_sample, state_mlstm_C, state_mlstm_n, state_mlstm_m, cache_conv, cache_ffn_conv, norm_mix_pre, norm_mix_post, norm_ffn_pre, norm_ffn_post, w_in, b_i, b_f, mlstm_norm, w_proj_a, conv_dw, conv_b, conv_ln_g, conv_ln_b, w_proj_b, b_proj_b, b_merge, w_out, w_up, ffn_dw, ffn_dw_b, w_down):
    assert b_i.shape[1] == NUM_HEADS
    depth = w_in.shape[0]
    mixer_p, ffn_p = _param_stacks(norm_mix_pre, norm_mix_post, norm_ffn_pre, norm_ffn_post, w_in, b_i, b_f,
                                   mlstm_norm, w_proj_a, conv_dw, conv_b, conv_ln_g, conv_ln_b, w_proj_b,
                                   b_proj_b, b_merge, w_out, w_up, ffn_dw, ffn_dw_b, w_down)
    m0 = jnp.broadcast_to(state_mlstm_m[..., None], state_mlstm_m.shape + (LANES,)).astype(F32)
    sample_init = (state_mlstm_C, state_mlstm_n, m0, cache_conv)
    yp, ys = x_prompt, x_sample
    prompt_states, sample_states = [], []
    for l in range(depth):
        yp, c1, n1, m1, cb1 = _mixer_layer(yp, None, mixer_p, l)
        yp, fb1 = _ffn_layer(yp, None, ffn_p, l)
        prompt_states.append((c1, n1, m1, cb1, fb1))
        ys, c2, n2, m2, cb2 = _mixer_layer(ys, sample_init, mixer_p, l)
        ys, fb2 = _ffn_layer(ys, cache_ffn_conv, ffn_p, l)
        sample_states.append((c2, n2, m2, cb2, fb2))
    pc, pn, pm, pcb, pfb = (jnp.stack(s) for s in zip(*prompt_states))
    sc, sn, sm, scb, sfb = (jnp.stack(s) for s in zip(*sample_states))
    return (yp, ys, pc, pn, pm[..., 0], pcb, pfb, sc, sn, sm[..., 0], scb, sfb)
```

```python
import functools
import math

import jax
import jax.numpy as jnp
from jax import lax
from jax.experimental import pallas as pl
from jax.experimental.pallas import tpu as pltpu

EPS = 1e-6
LOG2_E = 1.0 / math.log(2.0)
NUM_HEADS = 4
SUBLANES = 8
LANES = 128
CONV_HIST_ROWS = 32
FFN_HIST_ROWS = 8
BLOCK_POSITIONS = 256
MIXER_BLOCK_ROWS = 256
FFN_BLOCK_ROWS = 256
CONV_PART_ROWS = 128
VMEM_LIMIT_BYTES = 56 * 1024 * 1024
STATE_VMEM_BYTES = 8 * 1024 * 1024
PACK_BLOCK_BYTES = 6 * 1024 * 1024

F32 = jnp.float32
BF16 = jnp.bfloat16


def _dot(a, b):
    return jnp.dot(a, b, preferred_element_type=F32)


def _wdot(a, w_ref, rows=slice(None), cols=slice(None)):
    return _dot(a, pltpu.bitcast(w_ref[rows, cols], BF16))


def _dot_nt(a, b):
    return lax.dot_general(a, b, (((1,), (1,)), ((), ())), preferred_element_type=F32)


def _dot_tn(a, b):
    return lax.dot_general(a, b, (((0,), (0,)), ((), ())), preferred_element_type=F32)


def _sigmoid(x):
    return 1.0 / (1.0 + jnp.exp2(x * (-LOG2_E)))


def _log_sigmoid(x):
    return jnp.minimum(x, 0.0) - jnp.log1p(jnp.exp(-jnp.abs(x)))


def _rms_scale(x):
    return x * lax.rsqrt(jnp.mean(x * x, axis=-1, keepdims=True) + EPS)


def _split_bf16(x):
    hi = x.astype(BF16)
    lo = (x - hi.astype(F32)).astype(BF16)
    return hi, lo


def _mlstm_gates(xb, w_if_ref, w_ift_ref, b_if_row_ref, b_if_col_ref, tril, triu):
    g_col = _dot(xb, w_if_ref[...]) + b_if_row_ref[...]
    hi, lo = _split_bf16(_log_sigmoid(g_col))
    bcum_col = _dot(tril, hi) + _dot(tril, lo)
    rows_used = w_ift_ref.shape[0]
    if xb.shape[0] % LANES == 0:
        g_row = jnp.transpose(g_col)[:rows_used, :]
        bcum_row = jnp.transpose(bcum_col)[:rows_used, :]
    else:
        g_row = _dot_nt(w_ift_ref[...], xb) + b_if_col_ref[:, 0:1]
        hi, lo = _split_bf16(_log_sigmoid(g_row))
        bcum_row = _dot(hi, triu) + _dot(lo, triu)
    return g_col, bcum_col, g_row, bcum_row


def _mlstm_block(q, k, v, gates, h, causal, c_prev, n_prev, m_prev):
    tt = q.shape[0]
    g_col, bcum_col, g_row, bcum_row = gates
    ig_c = g_col[:, h:h + 1]
    bc = bcum_col[:, NUM_HEADS + h:NUM_HEADS + h + 1]
    ig_r = g_row[h:h + 1, :]
    br = bcum_row[NUM_HEADS + h:NUM_HEADS + h + 1, :]
    qb, kb, vb = q.astype(BF16), k.astype(BF16), v.astype(BF16)

    logd = jnp.where(causal, (bc - br) + ig_r, -jnp.inf)
    g = bc + m_prev
    m_tok = jnp.maximum(g, jnp.max(logd, axis=-1, keepdims=True))
    w = jnp.exp(logd - m_tok)
    inter = jnp.exp(g - m_tok)
    s = _dot_nt(qb, kb) * w
    num = _dot(s.astype(BF16), vb) + inter * _dot(qb, c_prev.astype(BF16))
    den = jnp.sum(s, axis=-1, keepdims=True) + inter * jnp.sum(q * n_prev, axis=-1, keepdims=True)
    hh = num * (1.0 / jnp.maximum(jnp.abs(den), jnp.exp(-m_tok)))

    m_new = m_tok[tt - 1:tt, :]
    decay = inter[tt - 1:tt, :]
    w_last = jnp.exp((bc[tt - 1:tt, :] - bc) + ig_c - m_new)
    c_new = decay * c_prev + _dot_tn(kb, (w_last * v).astype(BF16))
    n_new = decay * n_prev + jnp.sum(w_last * k, axis=0, keepdims=True)
    return hh, c_new, n_new, m_new


def _causal_conv_tile(glu_scr, conv_scr, conv_w_ref, conv_b_ref, b, tt, ls, hist0):
    conv_w = conv_w_ref.shape[0]
    part = min(tt, CONV_PART_ROWS)
    span = part + CONV_HIST_ROWS
    for p0 in range(0, tt, part):
        acc = jnp.broadcast_to(conv_b_ref[:, ls], (part, LANES))
        rows_in = glu_scr[b, p0:p0 + span, ls]
        for r in range(SUBLANES):
            offs = [o for o in range(hist0, hist0 + conv_w) if o % SUBLANES == r]
            window = rows_in if r == 0 else pltpu.roll(rows_in, span - r, axis=0)
            for o in offs:
                kk = o - hist0
                acc = acc + conv_w_ref[kk:kk + 1, ls] * window[o - r:o - r + part]
        conv_scr[b * tt + p0:b * tt + p0 + part, ls] = acc


def _mixer_kernel(*refs, nb, tt, has_init):
    if has_init:
        (x_ref, c0_ref, n0_ref, m0_ref, cb0_ref, *rest) = refs
    else:
        (x_ref, *rest) = refs
    (g_pre_ref, w_qkvo_ref, w_if_ref, w_ift_ref, b_if_row_ref, b_if_col_ref, mnorm_ref, w_pa_ref,
     w_u_ref, conv_w_ref, conv_b_ref, ln_g_ref, ln_b_ref, w_pb_ref, b_pb_ref, w_g_ref, b_merge_ref,
     w_out_ref, g_post_ref,
     y_ref, c_ref, n_ref, m_ref, cb_ref,
     xn_scr, hg_scr, glu_scr, conv_scr, gt_scr, gcol_scr, grow_scr) = rest

    t = pl.program_id(1)
    rows = nb * tt
    d_model = x_ref.shape[-1]
    d_a = mnorm_ref.shape[1]
    dh = d_a // NUM_HEADS
    d_b = conv_b_ref.shape[1]
    conv_w = conv_w_ref.shape[0]
    hist0 = CONV_HIST_ROWS - (conv_w - 1)

    @pl.when(t == 0)
    def _init_state():
        glu_scr[:, 0:CONV_HIST_ROWS, :] = jnp.zeros((nb, CONV_HIST_ROWS, d_b), F32)
        if has_init:
            c_ref[...] = c0_ref[...]
            n_ref[...] = n0_ref[...]
            m_ref[...] = m0_ref[...]
            glu_scr[:, hist0:CONV_HIST_ROWS, :] = cb0_ref[...]
        else:
            c_ref[...] = jnp.zeros(c_ref.shape, F32)
            n_ref[...] = jnp.zeros(n_ref.shape, F32)
            m_ref[...] = jnp.zeros(m_ref.shape, F32)

    x = x_ref[...].reshape(rows, d_model)
    xn_scr[...] = (_rms_scale(x) * g_pre_ref[...]).astype(BF16)

    row_id = lax.broadcasted_iota(jnp.int32, (tt, tt), 0)
    col_id = lax.broadcasted_iota(jnp.int32, (tt, tt), 1)
    tril = jnp.where(row_id >= col_id, 1.0, 0.0).astype(BF16)
    triu = jnp.where(col_id >= row_id, 1.0, 0.0).astype(BF16)
    for b in range(nb):
        g_col, bcum_col, g_row, bcum_row = _mlstm_gates(
            xn_scr[b * tt:(b + 1) * tt, :], w_if_ref, w_ift_ref, b_if_row_ref, b_if_col_ref, tril, triu)
        gcol_scr[b, 0], gcol_scr[b, 1] = g_col, bcum_col
        grow_scr[b, 0], grow_scr[b, 1] = g_row, bcum_row

    xn = xn_scr[...]
    u = _wdot(xn, w_u_ref)
    glu_scr[:, CONV_HIST_ROWS:CONV_HIST_ROWS + tt, :] = (u[:, :d_b] * _sigmoid(u[:, d_b:])).reshape(nb, tt, d_b)

    dg = d_model // NUM_HEADS
    causal = row_id >= col_id
    lane_tiles = d_b // LANES
    for h in range(NUM_HEADS):
        for j in range(h * lane_tiles // NUM_HEADS, (h + 1) * lane_tiles // NUM_HEADS):
            for b in range(nb):
                _causal_conv_tile(glu_scr, conv_scr, conv_w_ref, conv_b_ref, b, tt,
                                  slice(j * LANES, (j + 1) * LANES), hist0)
        hs = slice(h * dh, (h + 1) * dh)
        q_all = _wdot(xn, w_qkvo_ref, cols=slice(h * dh, (h + 1) * dh))
        k_all = _wdot(xn, w_qkvo_ref, cols=slice(d_a + h * dh, d_a + (h + 1) * dh)) * (1.0 / math.sqrt(dh))
        v_all = _wdot(xn, w_qkvo_ref, cols=slice(2 * d_a + h * dh, 2 * d_a + (h + 1) * dh))
        o_all = _wdot(xn, w_qkvo_ref, cols=slice(3 * d_a + h * dh, 3 * d_a + (h + 1) * dh))
        for b in range(nb):
            rs = slice(b * tt, (b + 1) * tt)
            gates = (gcol_scr[b, 0], gcol_scr[b, 1], grow_scr[b, 0], grow_scr[b, 1])
            hh, c_new, n_new, m_new = _mlstm_block(
                q_all[rs], k_all[rs], v_all[rs], gates, h, causal,
                c_ref[b, h], n_ref[b, h:h + 1, :], m_ref[b, h:h + 1, 0:1])
            c_ref[b, h] = c_new
            n_ref[b, h:h + 1, :] = n_new
            m_ref[b, h:h + 1, :] = jnp.broadcast_to(m_new, (1, LANES))
            hn = _rms_scale(hh) * mnorm_ref[:, hs]
            hg_scr[rs, hs] = (hn * _sigmoid(o_all[rs])).astype(BF16)

        for half in range(2):
            gs = slice(half * d_model + h * dg, half * d_model + (h + 1) * dg)
            gt_scr[:, gs] = _sigmoid(_wdot(xn, w_g_ref, cols=gs) + b_merge_ref[:, gs])
    glu_scr[:, 0:CONV_HIST_ROWS, :] = glu_scr[:, tt:tt + CONV_HIST_ROWS, :]

    if nb % 2 == 0:
        splits = [(slice(i * nb // 2, (i + 1) * nb // 2), slice(0, tt)) for i in range(2)]
    elif nb == 1 and tt % (4 * SUBLANES) == 0:
        splits = [(slice(0, 1), slice(i * tt // 2, (i + 1) * tt // 2)) for i in range(2)]
    else:
        splits = [(slice(0, nb), slice(0, tt))]
    part_rows = rows // len(splits)
    for i, (bs, ts) in enumerate(splits):
        rs = slice(i * part_rows, (i + 1) * part_rows)
        c = conv_scr[rs, :]
        xc = c - jnp.mean(c, axis=-1, keepdims=True)
        cn = xc * lax.rsqrt(jnp.mean(xc * xc, axis=-1, keepdims=True) + EPS) * ln_g_ref[...] + ln_b_ref[...]
        act = cn * _sigmoid(cn)
        yb = _wdot(act.astype(BF16), w_pb_ref) + b_pb_ref[...]
        ya = _wdot(hg_scr[rs, :], w_pa_ref)
        mix = gt_scr[rs, :d_model] * ya + gt_scr[rs, d_model:] * yb
        mo = _wdot(mix.astype(BF16), w_out_ref)
        y = x_ref[bs, ts, :].reshape(part_rows, d_model) + _rms_scale(mo) * g_post_ref[...]
        y_ref[bs, ts, :] = y.reshape(bs.stop - bs.start, ts.stop - ts.start, d_model)

    @pl.when(t == pl.num_programs(1) - 1)
    def _emit_conv_state():
        cb_ref[...] = glu_scr[:, hist0:CONV_HIST_ROWS, :]


def _ffn_kernel(*refs, nb, tt, has_init):
    if has_init:
        (x_ref, fb0_ref, *rest) = refs
    else:
        (x_ref, *rest) = refs
    (g_pre_ref, w_up_ref, dw_ref, dw_b_ref, w_down_ref, g_post_ref,
     y_ref, fb_ref, hist_scr) = rest

    t = pl.program_id(1)
    rows = nb * tt
    d_model = x_ref.shape[-1]
    d_ff = dw_ref.shape[1] // 2
    width = dw_ref.shape[0]
    hist0 = FFN_HIST_ROWS - (width - 1)

    @pl.when(t == 0)
    def _init_state():
        hist_scr[...] = jnp.zeros(hist_scr.shape, F32)
        if has_init:
            hist_scr[:, hist0:FFN_HIST_ROWS, :] = fb0_ref[...]

    x = x_ref[...].reshape(rows, d_model)
    hn = (_rms_scale(x) * g_pre_ref[...]).astype(BF16)
    up = _wdot(hn, w_up_ref)

    sub_row = lax.broadcasted_iota(jnp.int32, (SUBLANES, 2 * d_ff), 0)
    conv = dw_ref[width - 1:width, :] * up + dw_b_ref[...]
    for kk in range(width - 1):
        shift = width - 1 - kk
        moved = pltpu.roll(up, shift, axis=0)
        pieces = []
        for b in range(nb):
            head = jnp.where(sub_row < shift, pltpu.roll(hist_scr[b], shift, axis=0),
                             moved[b * tt:b * tt + SUBLANES])
            pieces += [head, moved[b * tt + SUBLANES:(b + 1) * tt]]
        conv = conv + dw_ref[kk:kk + 1, :] * jnp.concatenate(pieces, axis=0)
    for b in range(nb):
        hist_scr[b] = up[(b + 1) * tt - FFN_HIST_ROWS:(b + 1) * tt]

    gate = conv[:, :d_ff]
    lin = -2.0 * LOG2_E * math.sqrt(2.0 / math.pi)
    cdf = 1.0 / (1.0 + jnp.exp2(gate * (lin + (lin * 0.044715) * (gate * gate))))
    f = _wdot((gate * conv[:, d_ff:] * cdf).astype(BF16), w_down_ref)
    y = x_ref[...].reshape(rows, d_model) + _rms_scale(f) * g_post_ref[...]
    y_ref[...] = y.reshape(nb, tt, d_model)

    @pl.when(t == pl.num_programs(1) - 1)
    def _emit_conv_state():
        fb_ref[...] = hist_scr[:, hist0:FFN_HIST_ROWS, :]


def _block_rows(batch, seq, max_rows, bytes_per_batch_row=0):
    tt = BLOCK_POSITIONS if seq % BLOCK_POSITIONS == 0 else seq
    assert tt % (2 * SUBLANES) == 0 and tt >= CONV_HIST_ROWS, (batch, seq)
    nb = max(1, min(batch, max_rows // tt))
    if bytes_per_batch_row:
        nb = max(1, min(nb, STATE_VMEM_BYTES // bytes_per_batch_row))
    while batch % nb:
        nb -= 1
    return nb, tt


def _layer_spec(stacked, layer):
    zeros = (0,) * (stacked.ndim - 1)
    return pl.BlockSpec((None,) + stacked.shape[1:], lambda b, t: (layer,) + zeros, pipeline_mode=pl.Buffered(1))


def _state_spec(stacked_shape, layer, nb):
    zeros = (0,) * (len(stacked_shape) - 2)
    return pl.BlockSpec((None, nb) + tuple(stacked_shape[2:]), lambda b, t: (layer, b) + zeros)


def _per_batch(shape, nb):
    return pl.BlockSpec((nb,) + tuple(shape[1:]), lambda b, t: (b,) + (0,) * (len(shape) - 1))


MIXER_PARAMS = ('g_pre', 'w_qkvo', 'w_if', 'w_ift', 'b_if_row', 'b_if_col', 'mnorm', 'w_pa', 'w_u', 'conv_w',
                'conv_b', 'ln_g', 'ln_b', 'w_pb', 'b_pb', 'w_g', 'b_merge', 'w_out', 'g_post')
FFN_PARAMS = ('g_pre', 'w_up', 'dw', 'dw_b', 'w_down', 'g_post')


def _mixer_layer(x, init, p, layer):
    batch, seq, d_model = x.shape
    d_a = p['mnorm'].shape[-1]
    dh = d_a // NUM_HEADS
    d_b = p['conv_b'].shape[-1]
    conv_w = p['conv_w'].shape[-2]
    has_init = init is not None
    memory_bytes = NUM_HEADS * dh * dh * 4 * (4 if has_init else 2)
    nb, tt = _block_rows(batch, seq, MIXER_BLOCK_ROWS, memory_bytes)
    x_spec = pl.BlockSpec((nb, tt, d_model), lambda b, t: (b, t, 0))
    weights = [p[k] for k in MIXER_PARAMS]
    state_shapes = [(batch, NUM_HEADS, dh, dh), (batch, NUM_HEADS, dh), (batch, NUM_HEADS, LANES),
                    (batch, conv_w - 1, d_b)]
    operands = [x] + (list(init) if has_init else []) + weights
    in_specs = ([x_spec] + ([_state_spec(s.shape, layer, nb) for s in init] if has_init else [])
                + [_layer_spec(w, layer) for w in weights])
    out_shape = [jax.ShapeDtypeStruct(x.shape, F32)] + [jax.ShapeDtypeStruct(s, F32) for s in state_shapes]
    out_specs = [x_spec] + [_per_batch(s, nb) for s in state_shapes]
    rows = nb * tt
    return pl.pallas_call(
        functools.partial(_mixer_kernel, nb=nb, tt=tt, has_init=has_init),
        grid=(batch // nb, seq // tt),
        in_specs=in_specs, out_specs=out_specs, out_shape=out_shape,
        scratch_shapes=[pltpu.VMEM((rows, d_model), BF16), pltpu.VMEM((rows, d_a), BF16),
                        pltpu.VMEM((nb, CONV_HIST_ROWS + tt, d_b), F32),
                        pltpu.VMEM((rows, d_b), F32), pltpu.VMEM((rows, 2 * d_model), F32),
                        pltpu.VMEM((nb, 2, tt, LANES), F32),
                        pltpu.VMEM((nb, 2, SUBLANES, tt), F32)],
        compiler_params=pltpu.CompilerParams(dimension_semantics=("arbitrary", "arbitrary"),
                                             vmem_limit_bytes=VMEM_LIMIT_BYTES),
        name="mixer_init" if has_init else "mixer",
    )(*operands)


def _ffn_layer(x, init, p, layer):
    batch, seq, d_model = x.shape
    nb, tt = _block_rows(batch, seq, FFN_BLOCK_ROWS)
    d_ff = p['dw'].shape[-1] // 2
    width = p['dw'].shape[-2]
    has_init = init is not None
    x_spec = pl.BlockSpec((nb, tt, d_model), lambda b, t: (b, t, 0))
    weights = [p[k] for k in FFN_PARAMS]
    fb_shape = (batch, width - 1, 2 * d_ff)
    operands = [x] + ([init] if has_init else []) + weights
    in_specs = ([x_spec] + ([_state_spec(init.shape, layer, nb)] if has_init else [])
                + [_layer_spec(w, layer) for w in weights])
    return pl.pallas_call(
        functools.partial(_ffn_kernel, nb=nb, tt=tt, has_init=has_init),
        grid=(batch // nb, seq // tt),
        in_specs=in_specs,
        out_specs=[x_spec, _per_batch(fb_shape, nb)],
        out_shape=[jax.ShapeDtypeStruct(x.shape, F32), jax.ShapeDtypeStruct(fb_shape, F32)],
        scratch_shapes=[pltpu.VMEM((nb, FFN_HIST_ROWS, 2 * d_ff), F32)],
        compiler_params=pltpu.CompilerParams(dimension_semantics=("arbitrary", "arbitrary"),
                                             vmem_limit_bytes=VMEM_LIMIT_BYTES),
        name="ffn_init" if has_init else "ffn",
    )(*operands)


def _pack_kernel(w_ref, o_ref):
    o_ref[...] = pltpu.bitcast(w_ref[...].astype(BF16), jnp.uint32)


def _pack_weights(w, cols=None):
    depth, k, n = w.shape
    n = n if cols is None else cols
    tn = max(c for c in range(LANES, n + 1, LANES) if n % c == 0 and k * c * w.dtype.itemsize <= PACK_BLOCK_BYTES)
    return pl.pallas_call(
        _pack_kernel,
        grid=(depth, n // tn),
        in_specs=[pl.BlockSpec((None, k, tn), lambda l, j: (l, 0, j))],
        out_specs=pl.BlockSpec((None, k // 2, tn), lambda l, j: (l, 0, j)),
        out_shape=jax.ShapeDtypeStruct((depth, k // 2, n), jnp.uint32),
        name="pack_weights",
    )(w)


def _rows(v):
    return v.reshape(v.shape[0], 1, v.shape[1]).astype(F32)


def _param_stacks(norm_mix_pre, norm_mix_post, norm_ffn_pre, norm_ffn_post, w_in, b_i, b_f, mlstm_norm, w_proj_a,
                  conv_dw, conv_b, conv_ln_g, conv_ln_b, w_proj_b, b_proj_b, b_merge, w_out, w_up, ffn_dw,
                  ffn_dw_b, w_down):
    depth = w_in.shape[0]
    d_a = w_proj_a.shape[1]
    d_b = w_proj_b.shape[1]
    nh = b_i.shape[1]
    o_if = 4 * d_a
    o_u = o_if + 2 * nh
    o_g = o_u + 2 * d_b
    w_if = w_in[:, :, o_if:o_if + LANES].astype(BF16)
    b_if = jnp.concatenate([b_i, b_f], axis=1).astype(F32)
    mixer = {
        'g_pre': _rows(norm_mix_pre),
        'w_qkvo': _pack_weights(w_in, cols=o_if),
        'w_if': w_if,
        'w_ift': jnp.swapaxes(w_if, 1, 2)[:, :2 * nh, :],
        'b_if_row': jnp.pad(b_if, ((0, 0), (0, LANES - 2 * nh))).reshape(depth, 1, LANES),
        'b_if_col': jnp.broadcast_to(b_if[:, :, None], (depth, 2 * nh, LANES)),
        'mnorm': _rows(mlstm_norm),
        'w_pa': _pack_weights(w_proj_a),
        'w_u': _pack_weights(w_in[:, :, o_u:o_g].astype(BF16)),
        'conv_w': conv_dw.astype(F32),
        'conv_b': _rows(conv_b),
        'ln_g': _rows(conv_ln_g),
        'ln_b': _rows(conv_ln_b),
        'w_pb': _pack_weights(w_proj_b),
        'b_pb': _rows(b_proj_b),
        'w_g': _pack_weights(w_in[:, :, o_g:].astype(BF16)),
        'b_merge': _rows(b_merge),
        'w_out': _pack_weights(w_out),
        'g_post': _rows(norm_mix_post),
    }
    ffn = {
        'g_pre': _rows(norm_ffn_pre),
        'w_up': _pack_weights(w_up),
        'dw': ffn_dw.astype(F32),
        'dw_b': _rows(ffn_dw_b),
        'w_down': _pack_weights(w_down),
        'g_post': _rows(norm_ffn_post),
    }
    return mixer, ffn


def kernel(x_prompt, x_sample, state_mlstm_C, state_mlstm_n, state_mlstm_m, cache_conv, cache_ffn_conv, norm_mix_pre, norm_mix_post, norm_ffn_pre, norm_ffn_post, w_in, b_i, b_f, mlstm_norm, w_proj_a, conv_dw, conv_b, conv_ln_g, conv_ln_b, w_proj_b, b_proj_b, b_merge, w_out, w_up, ffn_dw, ffn_dw_b, w_down):
    assert b_i.shape[1] == NUM_HEADS
    depth = w_in.shape[0]
    mixer_p, ffn_p = _param_stacks(norm_mix_pre, norm_mix_post, norm_ffn_pre, norm_ffn_post, w_in, b_i, b_f,
                                   mlstm_norm, w_proj_a, conv_dw, conv_b, conv_ln_g, conv_ln_b, w_proj_b,
                                   b_proj_b, b_merge, w_out, w_up, ffn_dw, ffn_dw_b, w_down)
    m0 = jnp.broadcast_to(state_mlstm_m[..., None], state_mlstm_m.shape + (LANES,)).astype(F32)
    sample_init = (state_mlstm_C, state_mlstm_n, m0, cache_conv)
    yp, ys = x_prompt, x_sample
    prompt_states, sample_states = [], []
    for l in range(depth):
        yp, c1, n1, m1, cb1 = _mixer_layer(yp, None, mixer_p, l)
        yp, fb1 = _ffn_layer(yp, None, ffn_p, l)
        prompt_states.append((c1, n1, m1, cb1, fb1))
        ys, c2, n2, m2, cb2 = _mixer_layer(ys, sample_init, mixer_p, l)
        ys, fb2 = _ffn_layer(ys, cache_ffn_conv, ffn_p, l)
        sample_states.append((c2, n2, m2, cb2, fb2))
    pc, pn, pm, pcb, pfb = (jnp.stack(s) for s in zip(*prompt_states))
    sc, sn, sm, scb, sfb = (jnp.stack(s) for s in zip(*sample_states))
    return (yp, ys, pc, pn, pm[..., 0], pcb, pfb, sc, sn, sm[..., 0], scb, sfb)
```

```python
import functools
import math

import jax
import jax.numpy as jnp
from jax import lax
from jax.experimental import pallas as pl
from jax.experimental.pallas import tpu as pltpu

EPS = 1e-6
LOG2_E = 1.0 / math.log(2.0)
NUM_HEADS = 4
SUBLANES = 8
LANES = 128
CONV_HIST_ROWS = 32
FFN_HIST_ROWS = 8
BLOCK_POSITIONS = 256
MIXER_BLOCK_ROWS = 256
FFN_BLOCK_ROWS = 256
CONV_PART_ROWS = 128
VMEM_LIMIT_BYTES = 56 * 1024 * 1024
STATE_VMEM_BYTES = 8 * 1024 * 1024
PACK_BLOCK_BYTES = 6 * 1024 * 1024

F32 = jnp.float32
BF16 = jnp.bfloat16


def _dot(a, b):
    return jnp.dot(a, b, preferred_element_type=F32)


def _wdot(a, w_ref, rows=slice(None), cols=slice(None)):
    return _dot(a, pltpu.bitcast(w_ref[rows, cols], BF16))


def _dot_nt(a, b):
    return lax.dot_general(a, b, (((1,), (1,)), ((), ())), preferred_element_type=F32)


def _dot_tn(a, b):
    return lax.dot_general(a, b, (((0,), (0,)), ((), ())), preferred_element_type=F32)


def _sigmoid(x):
    return 1.0 / (1.0 + jnp.exp2(x * (-LOG2_E)))


def _log_sigmoid(x):
    return jnp.minimum(x, 0.0) - jnp.log1p(jnp.exp(-jnp.abs(x)))


def _rms_scale(x):
    return x * lax.rsqrt(jnp.mean(x * x, axis=-1, keepdims=True) + EPS)


def _split_bf16(x):
    hi = x.astype(BF16)
    lo = (x - hi.astype(F32)).astype(BF16)
    return hi, lo


def _mlstm_gates(xb, w_if_ref, w_ift_ref, b_if_row_ref, b_if_col_ref, tril, triu):
    g_col = _dot(xb, w_if_ref[...]) + b_if_row_ref[...]
    hi, lo = _split_bf16(_log_sigmoid(g_col))
    bcum_col = _dot(tril, hi) + _dot(tril, lo)
    rows_used = w_ift_ref.shape[0]
    if xb.shape[0] % LANES == 0:
        g_row = jnp.transpose(g_col)[:rows_used, :]
        bcum_row = jnp.transpose(bcum_col)[:rows_used, :]
    else:
        g_row = _dot_nt(w_ift_ref[...], xb) + b_if_col_ref[:, 0:1]
        hi, lo = _split_bf16(_log_sigmoid(g_row))
        bcum_row = _dot(hi, triu) + _dot(lo, triu)
    return g_col, bcum_col, g_row, bcum_row


def _mlstm_block(q, k, v, gates, h, causal, c_prev, n_prev, m_prev):
    tt = q.shape[0]
    g_col, bcum_col, g_row, bcum_row = gates
    ig_c = g_col[:, h:h + 1]
    bc = bcum_col[:, NUM_HEADS + h:NUM_HEADS + h + 1]
    ig_r = g_row[h:h + 1, :]
    br = bcum_row[NUM_HEADS + h:NUM_HEADS + h + 1, :]
    qb, kb, vb = q.astype(BF16), k.astype(BF16), v.astype(BF16)

    logd = jnp.where(causal, (bc - br) + ig_r, -jnp.inf)
    g = bc + m_prev
    m_tok = jnp.maximum(g, jnp.max(logd, axis=-1, keepdims=True))
    w = jnp.exp(logd - m_tok)
    inter = jnp.exp(g - m_tok)
    s = _dot_nt(qb, kb) * w
    num = _dot(s.astype(BF16), vb) + inter * _dot(qb, c_prev.astype(BF16))
    den = jnp.sum(s, axis=-1, keepdims=True) + inter * jnp.sum(q * n_prev, axis=-1, keepdims=True)
    hh = num * (1.0 / jnp.maximum(jnp.abs(den), jnp.exp(-m_tok)))

    m_new = m_tok[tt - 1:tt, :]
    decay = inter[tt - 1:tt, :]
    w_last = jnp.exp((bc[tt - 1:tt, :] - bc) + ig_c - m_new)
    c_new = decay * c_prev + _dot_tn(kb, (w_last * v).astype(BF16))
    n_new = decay * n_prev + jnp.sum(w_last * k, axis=0, keepdims=True)
    return hh, c_new, n_new, m_new


def _causal_conv_tile(glu_scr, conv_scr, conv_w_ref, conv_b_ref, b, tt, ls, hist0):
    conv_w = conv_w_ref.shape[0]
    part = min(tt, CONV_PART_ROWS)
    span = part + CONV_HIST_ROWS
    for p0 in range(0, tt, part):
        acc = jnp.broadcast_to(conv_b_ref[:, ls], (part, LANES))
        rows_in = glu_scr[b, p0:p0 + span, ls]
        for r in range(SUBLANES):
            offs = [o for o in range(hist0, hist0 + conv_w) if o % SUBLANES == r]
            window = rows_in if r == 0 else pltpu.roll(rows_in, span - r, axis=0)
            for o in offs:
                kk = o - hist0
                acc = acc + conv_w_ref[kk:kk + 1, ls] * window[o - r:o - r + part]
        conv_scr[b * tt + p0:b * tt + p0 + part, ls] = acc


def _mixer_kernel(*refs, nb, tt, has_init):
    if has_init:
        (x_ref, c0_ref, n0_ref, m0_ref, cb0_ref, *rest) = refs
    else:
        (x_ref, *rest) = refs
    (g_pre_ref, w_qkvo_ref, w_if_ref, w_ift_ref, b_if_row_ref, b_if_col_ref, mnorm_ref, w_pa_ref,
     w_u_ref, conv_w_ref, conv_b_ref, ln_g_ref, ln_b_ref, w_pb_ref, b_pb_ref, w_g_ref, b_merge_ref,
     w_out_ref, g_post_ref, _memory_stack_ref,
     y_ref, c_ref, n_ref, m_ref, cb_ref,
     xn_scr, hg_scr, glu_scr, conv_scr, gt_scr, gcol_scr, grow_scr) = rest

    t = pl.program_id(1)
    rows = nb * tt
    d_model = x_ref.shape[-1]
    d_a = mnorm_ref.shape[1]
    dh = d_a // NUM_HEADS
    d_b = conv_b_ref.shape[1]
    conv_w = conv_w_ref.shape[0]
    hist0 = CONV_HIST_ROWS - (conv_w - 1)

    @pl.when(t == 0)
    def _init_state():
        glu_scr[:, 0:CONV_HIST_ROWS, :] = jnp.zeros((nb, CONV_HIST_ROWS, d_b), F32)
        if has_init:
            c_ref[...] = c0_ref[...]
            n_ref[...] = n0_ref[...]
            m_ref[...] = m0_ref[...]
            glu_scr[:, hist0:CONV_HIST_ROWS, :] = cb0_ref[...]
        else:
            c_ref[...] = jnp.zeros(c_ref.shape, F32)
            n_ref[...] = jnp.zeros(n_ref.shape, F32)
            m_ref[...] = jnp.zeros(m_ref.shape, F32)

    x = x_ref[...].reshape(rows, d_model)
    xn_scr[...] = (_rms_scale(x) * g_pre_ref[...]).astype(BF16)

    row_id = lax.broadcasted_iota(jnp.int32, (tt, tt), 0)
    col_id = lax.broadcasted_iota(jnp.int32, (tt, tt), 1)
    tril = jnp.where(row_id >= col_id, 1.0, 0.0).astype(BF16)
    triu = jnp.where(col_id >= row_id, 1.0, 0.0).astype(BF16)
    for b in range(nb):
        g_col, bcum_col, g_row, bcum_row = _mlstm_gates(
            xn_scr[b * tt:(b + 1) * tt, :], w_if_ref, w_ift_ref, b_if_row_ref, b_if_col_ref, tril, triu)
        gcol_scr[b, 0], gcol_scr[b, 1] = g_col, bcum_col
        grow_scr[b, 0], grow_scr[b, 1] = g_row, bcum_row

    xn = xn_scr[...]
    u = _wdot(xn, w_u_ref)
    glu_scr[:, CONV_HIST_ROWS:CONV_HIST_ROWS + tt, :] = (u[:, :d_b] * _sigmoid(u[:, d_b:])).reshape(nb, tt, d_b)

    dg = d_model // NUM_HEADS
    causal = row_id >= col_id
    lane_tiles = d_b // LANES
    for h in range(NUM_HEADS):
        for j in range(h * lane_tiles // NUM_HEADS, (h + 1) * lane_tiles // NUM_HEADS):
            for b in range(nb):
                _causal_conv_tile(glu_scr, conv_scr, conv_w_ref, conv_b_ref, b, tt,
                                  slice(j * LANES, (j + 1) * LANES), hist0)
        hs = slice(h * dh, (h + 1) * dh)
        q_all = _wdot(xn, w_qkvo_ref, cols=slice(h * dh, (h + 1) * dh))
        k_all = _wdot(xn, w_qkvo_ref, cols=slice(d_a + h * dh, d_a + (h + 1) * dh)) * (1.0 / math.sqrt(dh))
        v_all = _wdot(xn, w_qkvo_ref, cols=slice(2 * d_a + h * dh, 2 * d_a + (h + 1) * dh))
        o_all = _wdot(xn, w_qkvo_ref, cols=slice(3 * d_a + h * dh, 3 * d_a + (h + 1) * dh))
        for b in range(nb):
            rs = slice(b * tt, (b + 1) * tt)
            gates = (gcol_scr[b, 0], gcol_scr[b, 1], grow_scr[b, 0], grow_scr[b, 1])
            hh, c_new, n_new, m_new = _mlstm_block(
                q_all[rs], k_all[rs], v_all[rs], gates, h, causal,
                c_ref[b, h], n_ref[b, h:h + 1, :], m_ref[b, h:h + 1, 0:1])
            c_ref[b, h] = c_new
            n_ref[b, h:h + 1, :] = n_new
            m_ref[b, h:h + 1, :] = jnp.broadcast_to(m_new, (1, LANES))
            hn = _rms_scale(hh) * mnorm_ref[:, hs]
            hg_scr[rs, hs] = (hn * _sigmoid(o_all[rs])).astype(BF16)

        for half in range(2):
            gs = slice(half * d_model + h * dg, half * d_model + (h + 1) * dg)
            gt_scr[:, gs] = _sigmoid(_wdot(xn, w_g_ref, cols=gs) + b_merge_ref[:, gs])
    glu_scr[:, 0:CONV_HIST_ROWS, :] = glu_scr[:, tt:tt + CONV_HIST_ROWS, :]

    if nb % 2 == 0:
        splits = [(slice(i * nb // 2, (i + 1) * nb // 2), slice(0, tt)) for i in range(2)]
    elif nb == 1 and tt % (4 * SUBLANES) == 0:
        splits = [(slice(0, 1), slice(i * tt // 2, (i + 1) * tt // 2)) for i in range(2)]
    else:
        splits = [(slice(0, nb), slice(0, tt))]
    part_rows = rows // len(splits)
    for i, (bs, ts) in enumerate(splits):
        rs = slice(i * part_rows, (i + 1) * part_rows)
        c = conv_scr[rs, :]
        xc = c - jnp.mean(c, axis=-1, keepdims=True)
        cn = xc * lax.rsqrt(jnp.mean(xc * xc, axis=-1, keepdims=True) + EPS) * ln_g_ref[...] + ln_b_ref[...]
        act = cn * _sigmoid(cn)
        yb = _wdot(act.astype(BF16), w_pb_ref) + b_pb_ref[...]
        ya = _wdot(hg_scr[rs, :], w_pa_ref)
        mix = gt_scr[rs, :d_model] * ya + gt_scr[rs, d_model:] * yb
        mo = _wdot(mix.astype(BF16), w_out_ref)
        y = x_ref[bs, ts, :].reshape(part_rows, d_model) + _rms_scale(mo) * g_post_ref[...]
        y_ref[bs, ts, :] = y.reshape(bs.stop - bs.start, ts.stop - ts.start, d_model)

    @pl.when(t == pl.num_programs(1) - 1)
    def _emit_conv_state():
        cb_ref[...] = glu_scr[:, hist0:CONV_HIST_ROWS, :]


def _ffn_kernel(*refs, nb, tt, has_init):
    if has_init:
        (x_ref, fb0_ref, *rest) = refs
    else:
        (x_ref, *rest) = refs
    (g_pre_ref, w_up_ref, dw_ref, dw_b_ref, w_down_ref, g_post_ref,
     y_ref, fb_ref, hist_scr) = rest

    t = pl.program_id(1)
    rows = nb * tt
    d_model = x_ref.shape[-1]
    d_ff = dw_ref.shape[1] // 2
    width = dw_ref.shape[0]
    hist0 = FFN_HIST_ROWS - (width - 1)

    @pl.when(t == 0)
    def _init_state():
        hist_scr[...] = jnp.zeros(hist_scr.shape, F32)
        if has_init:
            hist_scr[:, hist0:FFN_HIST_ROWS, :] = fb0_ref[...]

    x = x_ref[...].reshape(rows, d_model)
    hn = (_rms_scale(x) * g_pre_ref[...]).astype(BF16)
    up = _wdot(hn, w_up_ref)

    sub_row = lax.broadcasted_iota(jnp.int32, (SUBLANES, 2 * d_ff), 0)
    conv = dw_ref[width - 1:width, :] * up + dw_b_ref[...]
    for kk in range(width - 1):
        shift = width - 1 - kk
        moved = pltpu.roll(up, shift, axis=0)
        pieces = []
        for b in range(nb):
            head = jnp.where(sub_row < shift, pltpu.roll(hist_scr[b], shift, axis=0),
                             moved[b * tt:b * tt + SUBLANES])
            pieces += [head, moved[b * tt + SUBLANES:(b + 1) * tt]]
        conv = conv + dw_ref[kk:kk + 1, :] * jnp.concatenate(pieces, axis=0)
    for b in range(nb):
        hist_scr[b] = up[(b + 1) * tt - FFN_HIST_ROWS:(b + 1) * tt]

    gate = conv[:, :d_ff]
    lin = -2.0 * LOG2_E * math.sqrt(2.0 / math.pi)
    cdf = 1.0 / (1.0 + jnp.exp2(gate * (lin + (lin * 0.044715) * (gate * gate))))
    f = _wdot((gate * conv[:, d_ff:] * cdf).astype(BF16), w_down_ref)
    y = x_ref[...].reshape(rows, d_model) + _rms_scale(f) * g_post_ref[...]
    y_ref[...] = y.reshape(nb, tt, d_model)

    @pl.when(t == pl.num_programs(1) - 1)
    def _emit_conv_state():
        fb_ref[...] = hist_scr[:, hist0:FFN_HIST_ROWS, :]


def _block_rows(batch, seq, max_rows, bytes_per_batch_row=0):
    tt = BLOCK_POSITIONS if seq % BLOCK_POSITIONS == 0 else seq
    assert tt % (2 * SUBLANES) == 0 and tt >= CONV_HIST_ROWS, (batch, seq)
    nb = max(1, min(batch, max_rows // tt))
    if bytes_per_batch_row:
        nb = max(1, min(nb, STATE_VMEM_BYTES // bytes_per_batch_row))
    while batch % nb:
        nb -= 1
    return nb, tt


def _layer_spec(stacked, layer):
    zeros = (0,) * (stacked.ndim - 1)
    return pl.BlockSpec((None,) + stacked.shape[1:], lambda b, t: (layer,) + zeros, pipeline_mode=pl.Buffered(1))


def _state_spec(stacked_shape, layer, nb):
    zeros = (0,) * (len(stacked_shape) - 2)
    return pl.BlockSpec((None, nb) + tuple(stacked_shape[2:]), lambda b, t: (layer, b) + zeros)


def _per_batch(shape, nb):
    return pl.BlockSpec((nb,) + tuple(shape[1:]), lambda b, t: (b,) + (0,) * (len(shape) - 1))


MIXER_PARAMS = ('g_pre', 'w_qkvo', 'w_if', 'w_ift', 'b_if_row', 'b_if_col', 'mnorm', 'w_pa', 'w_u', 'conv_w',
                'conv_b', 'ln_g', 'ln_b', 'w_pb', 'b_pb', 'w_g', 'b_merge', 'w_out', 'g_post')
FFN_PARAMS = ('g_pre', 'w_up', 'dw', 'dw_b', 'w_down', 'g_post')


def _mixer_layer(x, init, p, layer, memory_stack):
    batch, seq, d_model = x.shape
    d_a = p['mnorm'].shape[-1]
    dh = d_a // NUM_HEADS
    d_b = p['conv_b'].shape[-1]
    conv_w = p['conv_w'].shape[-2]
    has_init = init is not None
    memory_bytes = NUM_HEADS * dh * dh * 4 * (4 if has_init else 2)
    nb, tt = _block_rows(batch, seq, MIXER_BLOCK_ROWS, memory_bytes)
    x_spec = pl.BlockSpec((nb, tt, d_model), lambda b, t: (b, t, 0))
    weights = [p[k] for k in MIXER_PARAMS]
    state_shapes = [(batch, NUM_HEADS, dh, dh), (batch, NUM_HEADS, dh), (batch, NUM_HEADS, LANES),
                    (batch, conv_w - 1, d_b)]
    operands = [x] + (list(init) if has_init else []) + weights + [memory_stack]
    in_specs = ([x_spec] + ([_state_spec(s.shape, layer, nb) for s in init] if has_init else [])
                + [_layer_spec(w, layer) for w in weights] + [pl.BlockSpec(memory_space=pl.ANY)])
    out_shape = ([jax.ShapeDtypeStruct(x.shape, F32), jax.ShapeDtypeStruct(memory_stack.shape, F32)]
                 + [jax.ShapeDtypeStruct(s, F32) for s in state_shapes[1:]])
    out_specs = ([x_spec, _state_spec(memory_stack.shape, layer, nb)]
                 + [_per_batch(s, nb) for s in state_shapes[1:]])
    rows = nb * tt
    return pl.pallas_call(
        functools.partial(_mixer_kernel, nb=nb, tt=tt, has_init=has_init),
        grid=(batch // nb, seq // tt),
        in_specs=in_specs, out_specs=out_specs, out_shape=out_shape,
        input_output_aliases={len(operands) - 1: 1},
        scratch_shapes=[pltpu.VMEM((rows, d_model), BF16), pltpu.VMEM((rows, d_a), BF16),
                        pltpu.VMEM((nb, CONV_HIST_ROWS + tt, d_b), F32),
                        pltpu.VMEM((rows, d_b), F32), pltpu.VMEM((rows, 2 * d_model), F32),
                        pltpu.VMEM((nb, 2, tt, LANES), F32),
                        pltpu.VMEM((nb, 2, SUBLANES, tt), F32)],
        compiler_params=pltpu.CompilerParams(dimension_semantics=("arbitrary", "arbitrary"),
                                             vmem_limit_bytes=VMEM_LIMIT_BYTES),
        name="mixer_init" if has_init else "mixer",
    )(*operands)


def _ffn_layer(x, init, p, layer):
    batch, seq, d_model = x.shape
    nb, tt = _block_rows(batch, seq, FFN_BLOCK_ROWS)
    d_ff = p['dw'].shape[-1] // 2
    width = p['dw'].shape[-2]
    has_init = init is not None
    x_spec = pl.BlockSpec((nb, tt, d_model), lambda b, t: (b, t, 0))
    weights = [p[k] for k in FFN_PARAMS]
    fb_shape = (batch, width - 1, 2 * d_ff)
    operands = [x] + ([init] if has_init else []) + weights
    in_specs = ([x_spec] + ([_state_spec(init.shape, layer, nb)] if has_init else [])
                + [_layer_spec(w, layer) for w in weights])
    return pl.pallas_call(
        functools.partial(_ffn_kernel, nb=nb, tt=tt, has_init=has_init),
        grid=(batch // nb, seq // tt),
        in_specs=in_specs,
        out_specs=[x_spec, _per_batch(fb_shape, nb)],
        out_shape=[jax.ShapeDtypeStruct(x.shape, F32), jax.ShapeDtypeStruct(fb_shape, F32)],
        scratch_shapes=[pltpu.VMEM((nb, FFN_HIST_ROWS, 2 * d_ff), F32)],
        compiler_params=pltpu.CompilerParams(dimension_semantics=("arbitrary", "arbitrary"),
                                             vmem_limit_bytes=VMEM_LIMIT_BYTES),
        name="ffn_init" if has_init else "ffn",
    )(*operands)


def _pack_kernel(w_ref, o_ref):
    o_ref[...] = pltpu.bitcast(w_ref[...].astype(BF16), jnp.uint32)


def _pack_weights(w, cols=None):
    depth, k, n = w.shape
    n = n if cols is None else cols
    tn = max(c for c in range(LANES, n + 1, LANES) if n % c == 0 and k * c * w.dtype.itemsize <= PACK_BLOCK_BYTES)
    return pl.pallas_call(
        _pack_kernel,
        grid=(depth, n // tn),
        in_specs=[pl.BlockSpec((None, k, tn), lambda l, j: (l, 0, j))],
        out_specs=pl.BlockSpec((None, k // 2, tn), lambda l, j: (l, 0, j)),
        out_shape=jax.ShapeDtypeStruct((depth, k // 2, n), jnp.uint32),
        name="pack_weights",
    )(w)


def _rows(v):
    return v.reshape(v.shape[0], 1, v.shape[1]).astype(F32)


def _param_stacks(norm_mix_pre, norm_mix_post, norm_ffn_pre, norm_ffn_post, w_in, b_i, b_f, mlstm_norm, w_proj_a,
                  conv_dw, conv_b, conv_ln_g, conv_ln_b, w_proj_b, b_proj_b, b_merge, w_out, w_up, ffn_dw,
                  ffn_dw_b, w_down):
    depth = w_in.shape[0]
    d_a = w_proj_a.shape[1]
    d_b = w_proj_b.shape[1]
    nh = b_i.shape[1]
    o_if = 4 * d_a
    o_u = o_if + 2 * nh
    o_g = o_u + 2 * d_b
    w_if = w_in[:, :, o_if:o_if + LANES].astype(BF16)
    b_if = jnp.concatenate([b_i, b_f], axis=1).astype(F32)
    mixer = {
        'g_pre': _rows(norm_mix_pre),
        'w_qkvo': _pack_weights(w_in, cols=o_if),
        'w_if': w_if,
        'w_ift': jnp.swapaxes(w_if, 1, 2)[:, :2 * nh, :],
        'b_if_row': jnp.pad(b_if, ((0, 0), (0, LANES - 2 * nh))).reshape(depth, 1, LANES),
        'b_if_col': jnp.broadcast_to(b_if[:, :, None], (depth, 2 * nh, LANES)),
        'mnorm': _rows(mlstm_norm),
        'w_pa': _pack_weights(w_proj_a),
        'w_u': _pack_weights(w_in[:, :, o_u:o_g].astype(BF16)),
        'conv_w': conv_dw.astype(F32),
        'conv_b': _rows(conv_b),
        'ln_g': _rows(conv_ln_g),
        'ln_b': _rows(conv_ln_b),
        'w_pb': _pack_weights(w_proj_b),
        'b_pb': _rows(b_proj_b),
        'w_g': _pack_weights(w_in[:, :, o_g:].astype(BF16)),
        'b_merge': _rows(b_merge),
        'w_out': _pack_weights(w_out),
        'g_post': _rows(norm_mix_post),
    }
    ffn = {
        'g_pre': _rows(norm_ffn_pre),
        'w_up': _pack_weights(w_up),
        'dw': ffn_dw.astype(F32),
        'dw_b': _rows(ffn_dw_b),
        'w_down': _pack_weights(w_down),
        'g_post': _rows(norm_ffn_post),
    }
    return mixer, ffn


def kernel(x_prompt, x_sample, state_mlstm_C, state_mlstm_n, state_mlstm_m, cache_conv, cache_ffn_conv, norm_mix_pre, norm_mix_post, norm_ffn_pre, norm_ffn_post, w_in, b_i, b_f, mlstm_norm, w_proj_a, conv_dw, conv_b, conv_ln_g, conv_ln_b, w_proj_b, b_proj_b, b_merge, w_out, w_up, ffn_dw, ffn_dw_b, w_down):
    assert b_i.shape[1] == NUM_HEADS
    depth = w_in.shape[0]
    mixer_p, ffn_p = _param_stacks(norm_mix_pre, norm_mix_post, norm_ffn_pre, norm_ffn_post, w_in, b_i, b_f,
                                   mlstm_norm, w_proj_a, conv_dw, conv_b, conv_ln_g, conv_ln_b, w_proj_b,
                                   b_proj_b, b_merge, w_out, w_up, ffn_dw, ffn_dw_b, w_down)
    m0 = jnp.broadcast_to(state_mlstm_m[..., None], state_mlstm_m.shape + (LANES,)).astype(F32)
    sample_init = (state_mlstm_C, state_mlstm_n, m0, cache_conv)
    yp, ys = x_prompt, x_sample
    pc = jnp.zeros((depth, x_prompt.shape[0]) + state_mlstm_C.shape[2:], F32)
    sc = jnp.zeros(state_mlstm_C.shape, F32)
    prompt_states, sample_states = [], []
    for l in range(depth):
        yp, pc, n1, m1, cb1 = _mixer_layer(yp, None, mixer_p, l, pc)
        yp, fb1 = _ffn_layer(yp, None, ffn_p, l)
        prompt_states.append((n1, m1, cb1, fb1))
        ys, sc, n2, m2, cb2 = _mixer_layer(ys, sample_init, mixer_p, l, sc)
        ys, fb2 = _ffn_layer(ys, cache_ffn_conv, ffn_p, l)
        sample_states.append((n2, m2, cb2, fb2))
    pn, pm, pcb, pfb = (jnp.stack(s) for s in zip(*prompt_states))
    sn, sm, scb, sfb = (jnp.stack(s) for s in zip(*sample_states))
    return (yp, ys, pc, pn, pm[..., 0], pcb, pfb, sc, sn, sm[..., 0], scb, sfb)
```

```python
import functools
import math

import jax
import jax.numpy as jnp
from jax import lax
from jax.experimental import pallas as pl
from jax.experimental.pallas import tpu as pltpu

EPS = 1e-6
LOG2_E = 1.0 / math.log(2.0)
NUM_HEADS = 4
SUBLANES = 8
LANES = 128
CONV_HIST_ROWS = 32
FFN_HIST_ROWS = 8
BLOCK_POSITIONS = 256
MIXER_BLOCK_ROWS = 256
FFN_BLOCK_ROWS = 256
CONV_PART_ROWS = 128
VMEM_LIMIT_BYTES = 56 * 1024 * 1024
STATE_VMEM_BYTES = 8 * 1024 * 1024
PACK_BLOCK_BYTES = 6 * 1024 * 1024

F32 = jnp.float32
BF16 = jnp.bfloat16


def _dot(a, b):
    return jnp.dot(a, b, preferred_element_type=F32)


def _wdot(a, w_ref, rows=slice(None), cols=slice(None)):
    return _dot(a, pltpu.bitcast(w_ref[rows, cols], BF16))


def _dot_nt(a, b):
    return lax.dot_general(a, b, (((1,), (1,)), ((), ())), preferred_element_type=F32)


def _dot_tn(a, b):
    return lax.dot_general(a, b, (((0,), (0,)), ((), ())), preferred_element_type=F32)


def _sigmoid(x):
    return 1.0 / (1.0 + jnp.exp2(x * (-LOG2_E)))


def _log_sigmoid(x):
    return jnp.minimum(x, 0.0) - jnp.log1p(jnp.exp(-jnp.abs(x)))


def _rms_scale(x):
    return x * lax.rsqrt(jnp.mean(x * x, axis=-1, keepdims=True) + EPS)


def _split_bf16(x):
    hi = x.astype(BF16)
    lo = (x - hi.astype(F32)).astype(BF16)
    return hi, lo


def _mlstm_gates(xb, w_if_ref, w_ift_ref, b_if_row_ref, b_if_col_ref, tril, triu):
    g_col = _dot(xb, w_if_ref[...]) + b_if_row_ref[...]
    hi, lo = _split_bf16(_log_sigmoid(g_col))
    bcum_col = _dot(tril, hi) + _dot(tril, lo)
    rows_used = w_ift_ref.shape[0]
    if xb.shape[0] % LANES == 0:
        g_row = jnp.transpose(g_col)[:rows_used, :]
        bcum_row = jnp.transpose(bcum_col)[:rows_used, :]
    else:
        g_row = _dot_nt(w_ift_ref[...], xb) + b_if_col_ref[:, 0:1]
        hi, lo = _split_bf16(_log_sigmoid(g_row))
        bcum_row = _dot(hi, triu) + _dot(lo, triu)
    return g_col, bcum_col, g_row, bcum_row


def _mlstm_block(q, k, v, gates, h, causal, c_prev, n_prev, m_prev):
    tt = q.shape[0]
    g_col, bcum_col, g_row, bcum_row = gates
    ig_c = g_col[:, h:h + 1]
    bc = bcum_col[:, NUM_HEADS + h:NUM_HEADS + h + 1]
    ig_r = g_row[h:h + 1, :]
    br = bcum_row[NUM_HEADS + h:NUM_HEADS + h + 1, :]
    qb, kb, vb = q.astype(BF16), k.astype(BF16), v.astype(BF16)

    logd = jnp.where(causal, (bc - br) + ig_r, -jnp.inf)
    g = bc + m_prev
    m_tok = jnp.maximum(g, jnp.max(logd, axis=-1, keepdims=True))
    w = jnp.exp(logd - m_tok)
    inter = jnp.exp(g - m_tok)
    s = _dot_nt(qb, kb) * w
    num = _dot(s.astype(BF16), vb) + inter * _dot(qb, c_prev.astype(BF16))
    den = jnp.sum(s, axis=-1, keepdims=True) + inter * jnp.sum(q * n_prev, axis=-1, keepdims=True)
    hh = num * (1.0 / jnp.maximum(jnp.abs(den), jnp.exp(-m_tok)))

    m_new = m_tok[tt - 1:tt, :]
    decay = inter[tt - 1:tt, :]
    w_last = jnp.exp((bc[tt - 1:tt, :] - bc) + ig_c - m_new)
    c_new = decay * c_prev + _dot_tn(kb, (w_last * v).astype(BF16))
    n_new = decay * n_prev + jnp.sum(w_last * k, axis=0, keepdims=True)
    return hh, c_new, n_new, m_new


def _causal_conv_tile(glu_scr, conv_scr, conv_w_ref, conv_b_ref, b, tt, ls, hist0):
    conv_w = conv_w_ref.shape[0]
    part = min(tt, CONV_PART_ROWS)
    span = part + CONV_HIST_ROWS
    for p0 in range(0, tt, part):
        acc = jnp.broadcast_to(conv_b_ref[:, ls], (part, LANES))
        rows_in = glu_scr[b, p0:p0 + span, ls]
        for r in range(SUBLANES):
            offs = [o for o in range(hist0, hist0 + conv_w) if o % SUBLANES == r]
            window = rows_in if r == 0 else pltpu.roll(rows_in, span - r, axis=0)
            for o in offs:
                kk = o - hist0
                acc = acc + conv_w_ref[kk:kk + 1, ls] * window[o - r:o - r + part]
        conv_scr[b * tt + p0:b * tt + p0 + part, ls] = acc


def _mixer_kernel(*refs, nb, tt, has_init):
    if has_init:
        (x_ref, c0_ref, n0_ref, m0_ref, cb0_ref, *rest) = refs
    else:
        (x_ref, *rest) = refs
    (g_pre_ref, w_qkvo_ref, w_if_ref, w_ift_ref, b_if_row_ref, b_if_col_ref, mnorm_ref, w_pa_ref,
     w_u_ref, conv_w_ref, conv_b_ref, ln_g_ref, ln_b_ref, w_pb_ref, b_pb_ref, w_g_ref, b_merge_ref,
     w_out_ref, g_post_ref, _memory_stack_ref,
     y_ref, c_ref, n_ref, m_ref, cb_ref,
     xn_scr, hg_scr, glu_scr, conv_scr, gt_scr, gcol_scr, grow_scr) = rest

    t = pl.program_id(1)
    rows = nb * tt
    d_model = x_ref.shape[-1]
    d_a = mnorm_ref.shape[1]
    dh = d_a // NUM_HEADS
    d_b = conv_b_ref.shape[1]
    conv_w = conv_w_ref.shape[0]
    hist0 = CONV_HIST_ROWS - (conv_w - 1)

    @pl.when(t == 0)
    def _init_state():
        glu_scr[:, 0:CONV_HIST_ROWS, :] = jnp.zeros((nb, CONV_HIST_ROWS, d_b), F32)
        if has_init:
            c_ref[...] = c0_ref[...]
            n_ref[...] = n0_ref[...]
            m_ref[...] = m0_ref[...]
            glu_scr[:, hist0:CONV_HIST_ROWS, :] = cb0_ref[...]
        else:
            c_ref[...] = jnp.zeros(c_ref.shape, F32)
            n_ref[...] = jnp.zeros(n_ref.shape, F32)
            m_ref[...] = jnp.zeros(m_ref.shape, F32)

    x = x_ref[...].reshape(rows, d_model)
    xn_scr[...] = (_rms_scale(x) * g_pre_ref[...]).astype(BF16)

    row_id = lax.broadcasted_iota(jnp.int32, (tt, tt), 0)
    col_id = lax.broadcasted_iota(jnp.int32, (tt, tt), 1)
    tril = jnp.where(row_id >= col_id, 1.0, 0.0).astype(BF16)
    triu = jnp.where(col_id >= row_id, 1.0, 0.0).astype(BF16)
    for b in range(nb):
        g_col, bcum_col, g_row, bcum_row = _mlstm_gates(
            xn_scr[b * tt:(b + 1) * tt, :], w_if_ref, w_ift_ref, b_if_row_ref, b_if_col_ref, tril, triu)
        gcol_scr[b, 0], gcol_scr[b, 1] = g_col, bcum_col
        grow_scr[b, 0], grow_scr[b, 1] = g_row, bcum_row

    xn = xn_scr[...]
    u = _wdot(xn, w_u_ref)
    glu_scr[:, CONV_HIST_ROWS:CONV_HIST_ROWS + tt, :] = (u[:, :d_b] * _sigmoid(u[:, d_b:])).reshape(nb, tt, d_b)

    dg = d_model // NUM_HEADS
    causal = row_id >= col_id
    lane_tiles = d_b // LANES
    for h in range(NUM_HEADS):
        for j in range(h * lane_tiles // NUM_HEADS, (h + 1) * lane_tiles // NUM_HEADS):
            for b in range(nb):
                _causal_conv_tile(glu_scr, conv_scr, conv_w_ref, conv_b_ref, b, tt,
                                  slice(j * LANES, (j + 1) * LANES), hist0)
        hs = slice(h * dh, (h + 1) * dh)
        q_all = _wdot(xn, w_qkvo_ref, cols=slice(h * dh, (h + 1) * dh))
        k_all = _wdot(xn, w_qkvo_ref, cols=slice(d_a + h * dh, d_a + (h + 1) * dh)) * (1.0 / math.sqrt(dh))
        v_all = _wdot(xn, w_qkvo_ref, cols=slice(2 * d_a + h * dh, 2 * d_a + (h + 1) * dh))
        o_all = _wdot(xn, w_qkvo_ref, cols=slice(3 * d_a + h * dh, 3 * d_a + (h + 1) * dh))
        for b in range(nb):
            rs = slice(b * tt, (b + 1) * tt)
            gates = (gcol_scr[b, 0], gcol_scr[b, 1], grow_scr[b, 0], grow_scr[b, 1])
            hh, c_new, n_new, m_new = _mlstm_block(
                q_all[rs], k_all[rs], v_all[rs], gates, h, causal,
                c_ref[b, h], n_ref[b, h:h + 1, :], m_ref[b, h:h + 1, 0:1])
            c_ref[b, h] = c_new
            n_ref[b, h:h + 1, :] = n_new
            m_ref[b, h:h + 1, :] = jnp.broadcast_to(m_new, (1, LANES))
            hn = _rms_scale(hh) * mnorm_ref[:, hs]
            hg_scr[rs, hs] = (hn * _sigmoid(o_all[rs])).astype(BF16)

        for half in range(2):
            gs = slice(half * d_model + h * dg, half * d_model + (h + 1) * dg)
            gt_scr[:, gs] = _sigmoid(_wdot(xn, w_g_ref, cols=gs) + b_merge_ref[:, gs])
    glu_scr[:, 0:CONV_HIST_ROWS, :] = glu_scr[:, tt:tt + CONV_HIST_ROWS, :]

    if nb % 2 == 0:
        splits = [(slice(i * nb // 2, (i + 1) * nb // 2), slice(0, tt)) for i in range(2)]
    elif nb == 1 and tt % (4 * SUBLANES) == 0:
        splits = [(slice(0, 1), slice(i * tt // 2, (i + 1) * tt // 2)) for i in range(2)]
    else:
        splits = [(slice(0, nb), slice(0, tt))]
    part_rows = rows // len(splits)
    for i, (bs, ts) in enumerate(splits):
        rs = slice(i * part_rows, (i + 1) * part_rows)
        c = conv_scr[rs, :]
        xc = c - jnp.mean(c, axis=-1, keepdims=True)
        cn = xc * lax.rsqrt(jnp.mean(xc * xc, axis=-1, keepdims=True) + EPS) * ln_g_ref[...] + ln_b_ref[...]
        act = cn * _sigmoid(cn)
        yb = _wdot(act.astype(BF16), w_pb_ref) + b_pb_ref[...]
        ya = _wdot(hg_scr[rs, :], w_pa_ref)
        mix = gt_scr[rs, :d_model] * ya + gt_scr[rs, d_model:] * yb
        mo = _wdot(mix.astype(BF16), w_out_ref)
        y = x_ref[bs, ts, :].reshape(part_rows, d_model) + _rms_scale(mo) * g_post_ref[...]
        y_ref[bs, ts, :] = y.reshape(bs.stop - bs.start, ts.stop - ts.start, d_model)

    @pl.when(t == pl.num_programs(1) - 1)
    def _emit_conv_state():
        cb_ref[...] = glu_scr[:, hist0:CONV_HIST_ROWS, :]


def _ffn_kernel(*refs, nb, tt, has_init):
    if has_init:
        (x_ref, fb0_ref, *rest) = refs
    else:
        (x_ref, *rest) = refs
    (g_pre_ref, w_up_ref, dw_ref, dw_b_ref, w_down_ref, g_post_ref,
     y_ref, fb_ref, hist_scr) = rest

    t = pl.program_id(1)
    rows = nb * tt
    d_model = x_ref.shape[-1]
    d_ff = dw_ref.shape[1] // 2
    width = dw_ref.shape[0]
    hist0 = FFN_HIST_ROWS - (width - 1)

    @pl.when(t == 0)
    def _init_state():
        hist_scr[...] = jnp.zeros(hist_scr.shape, F32)
        if has_init:
            hist_scr[:, hist0:FFN_HIST_ROWS, :] = fb0_ref[...]

    x = x_ref[...].reshape(rows, d_model)
    hn = (_rms_scale(x) * g_pre_ref[...]).astype(BF16)
    up = _wdot(hn, w_up_ref)

    sub_row = lax.broadcasted_iota(jnp.int32, (SUBLANES, 2 * d_ff), 0)
    conv = dw_ref[width - 1:width, :] * up + dw_b_ref[...]
    for kk in range(width - 1):
        shift = width - 1 - kk
        moved = pltpu.roll(up, shift, axis=0)
        pieces = []
        for b in range(nb):
            head = jnp.where(sub_row < shift, pltpu.roll(hist_scr[b], shift, axis=0),
                             moved[b * tt:b * tt + SUBLANES])
            pieces += [head, moved[b * tt + SUBLANES:(b + 1) * tt]]
        conv = conv + dw_ref[kk:kk + 1, :] * jnp.concatenate(pieces, axis=0)
    for b in range(nb):
        hist_scr[b] = up[(b + 1) * tt - FFN_HIST_ROWS:(b + 1) * tt]

    gate = conv[:, :d_ff]
    lin = -2.0 * LOG2_E * math.sqrt(2.0 / math.pi)
    cdf = 1.0 / (1.0 + jnp.exp2(gate * (lin + (lin * 0.044715) * (gate * gate))))
    f = _wdot((gate * conv[:, d_ff:] * cdf).astype(BF16), w_down_ref)
    y = x_ref[...].reshape(rows, d_model) + _rms_scale(f) * g_post_ref[...]
    y_ref[...] = y.reshape(nb, tt, d_model)

    @pl.when(t == pl.num_programs(1) - 1)
    def _emit_conv_state():
        fb_ref[...] = hist_scr[:, hist0:FFN_HIST_ROWS, :]


def _block_rows(batch, seq, max_rows, bytes_per_batch_row=0):
    tt = BLOCK_POSITIONS if seq % BLOCK_POSITIONS == 0 else seq
    assert tt % (2 * SUBLANES) == 0 and tt >= CONV_HIST_ROWS, (batch, seq)
    nb = max(1, min(batch, max_rows // tt))
    if bytes_per_batch_row:
        nb = max(1, min(nb, STATE_VMEM_BYTES // bytes_per_batch_row))
    while batch % nb:
        nb -= 1
    return nb, tt


def _layer_spec(stacked, layer):
    zeros = (0,) * (stacked.ndim - 1)
    return pl.BlockSpec((None,) + stacked.shape[1:], lambda b, t: (layer,) + zeros, pipeline_mode=pl.Buffered(1))


def _state_spec(stacked_shape, layer, nb):
    zeros = (0,) * (len(stacked_shape) - 2)
    return pl.BlockSpec((None, nb) + tuple(stacked_shape[2:]), lambda b, t: (layer, b) + zeros)


def _per_batch(shape, nb):
    return pl.BlockSpec((nb,) + tuple(shape[1:]), lambda b, t: (b,) + (0,) * (len(shape) - 1))


MIXER_PARAMS = ('g_pre', 'w_qkvo', 'w_if', 'w_ift', 'b_if_row', 'b_if_col', 'mnorm', 'w_pa', 'w_u', 'conv_w',
                'conv_b', 'ln_g', 'ln_b', 'w_pb', 'b_pb', 'w_g', 'b_merge', 'w_out', 'g_post')
FFN_PARAMS = ('g_pre', 'w_up', 'dw', 'dw_b', 'w_down', 'g_post')


def _mixer_layer(x, init, p, layer, memory_stack):
    batch, seq, d_model = x.shape
    d_a = p['mnorm'].shape[-1]
    dh = d_a // NUM_HEADS
    d_b = p['conv_b'].shape[-1]
    conv_w = p['conv_w'].shape[-2]
    has_init = init is not None
    memory_bytes = NUM_HEADS * dh * dh * 4 * (4 if has_init else 2)
    nb, tt = _block_rows(batch, seq, MIXER_BLOCK_ROWS, memory_bytes)
    x_spec = pl.BlockSpec((nb, tt, d_model), lambda b, t: (b, t, 0))
    weights = [p[k] for k in MIXER_PARAMS]
    state_shapes = [(batch, NUM_HEADS, dh, dh), (batch, NUM_HEADS, dh), (batch, NUM_HEADS, LANES),
                    (batch, conv_w - 1, d_b)]
    operands = [x] + (list(init) if has_init else []) + weights + [memory_stack]
    in_specs = ([x_spec] + ([_state_spec(s.shape, layer, nb) for s in init] if has_init else [])
                + [_layer_spec(w, layer) for w in weights] + [pl.BlockSpec(memory_space=pl.ANY)])
    out_shape = ([jax.ShapeDtypeStruct(x.shape, F32), jax.ShapeDtypeStruct(memory_stack.shape, F32)]
                 + [jax.ShapeDtypeStruct(s, F32) for s in state_shapes[1:]])
    out_specs = ([x_spec, _state_spec(memory_stack.shape, layer, nb)]
                 + [_per_batch(s, nb) for s in state_shapes[1:]])
    rows = nb * tt
    return pl.pallas_call(
        functools.partial(_mixer_kernel, nb=nb, tt=tt, has_init=has_init),
        grid=(batch // nb, seq // tt),
        in_specs=in_specs, out_specs=out_specs, out_shape=out_shape,
        input_output_aliases={len(operands) - 1: 1},
        scratch_shapes=[pltpu.VMEM((rows, d_model), BF16), pltpu.VMEM((rows, d_a), BF16),
                        pltpu.VMEM((nb, CONV_HIST_ROWS + tt, d_b), F32),
                        pltpu.VMEM((rows, d_b), F32), pltpu.VMEM((rows, 2 * d_model), F32),
                        pltpu.VMEM((nb, 2, tt, LANES), F32),
                        pltpu.VMEM((nb, 2, SUBLANES, tt), F32)],
        compiler_params=pltpu.CompilerParams(dimension_semantics=("arbitrary", "arbitrary"),
                                             vmem_limit_bytes=VMEM_LIMIT_BYTES),
        name="mixer_init" if has_init else "mixer",
    )(*operands)


def _ffn_layer(x, init, p, layer):
    batch, seq, d_model = x.shape
    nb, tt = _block_rows(batch, seq, FFN_BLOCK_ROWS)
    d_ff = p['dw'].shape[-1] // 2
    width = p['dw'].shape[-2]
    has_init = init is not None
    x_spec = pl.BlockSpec((nb, tt, d_model), lambda b, t: (b, t, 0))
    weights = [p[k] for k in FFN_PARAMS]
    fb_shape = (batch, width - 1, 2 * d_ff)
    operands = [x] + ([init] if has_init else []) + weights
    in_specs = ([x_spec] + ([_state_spec(init.shape, layer, nb)] if has_init else [])
                + [_layer_spec(w, layer) for w in weights])
    return pl.pallas_call(
        functools.partial(_ffn_kernel, nb=nb, tt=tt, has_init=has_init),
        grid=(batch // nb, seq // tt),
        in_specs=in_specs,
        out_specs=[x_spec, _per_batch(fb_shape, nb)],
        out_shape=[jax.ShapeDtypeStruct(x.shape, F32), jax.ShapeDtypeStruct(fb_shape, F32)],
        scratch_shapes=[pltpu.VMEM((nb, FFN_HIST_ROWS, 2 * d_ff), F32)],
        compiler_params=pltpu.CompilerParams(dimension_semantics=("arbitrary", "arbitrary"),
                                             vmem_limit_bytes=VMEM_LIMIT_BYTES),
        name="ffn_init" if has_init else "ffn",
    )(*operands)


def _pack_kernel(w_ref, o_ref):
    o_ref[...] = pltpu.bitcast(w_ref[...].astype(BF16), jnp.uint32)


def _pack_weights(w):
    depth, k, n = w.shape
    tn = max(c for c in range(LANES, n + 1, LANES) if n % c == 0 and k * c * w.dtype.itemsize <= PACK_BLOCK_BYTES)
    return pl.pallas_call(
        _pack_kernel,
        grid=(depth, n // tn),
        in_specs=[pl.BlockSpec((None, k, tn), lambda l, j: (l, 0, j))],
        out_specs=pl.BlockSpec((None, k // 2, tn), lambda l, j: (l, 0, j)),
        out_shape=jax.ShapeDtypeStruct((depth, k // 2, n), jnp.uint32),
        name="pack_weights",
    )(w)


def _rows(v):
    return v.reshape(v.shape[0], 1, v.shape[1]).astype(F32)


def _param_stacks(norm_mix_pre, norm_mix_post, norm_ffn_pre, norm_ffn_post, w_in, b_i, b_f, mlstm_norm, w_proj_a,
                  conv_dw, conv_b, conv_ln_g, conv_ln_b, w_proj_b, b_proj_b, b_merge, w_out, w_up, ffn_dw,
                  ffn_dw_b, w_down):
    depth = w_in.shape[0]
    d_a = w_proj_a.shape[1]
    d_b = w_proj_b.shape[1]
    nh = b_i.shape[1]
    o_if = 4 * d_a
    o_u = o_if + 2 * nh
    o_g = o_u + 2 * d_b
    w_if = w_in[:, :, o_if:o_if + LANES].astype(BF16)
    b_if = jnp.concatenate([b_i, b_f], axis=1).astype(F32)
    mixer = {
        'g_pre': _rows(norm_mix_pre),
        'w_qkvo': _pack_weights(w_in[:, :, :o_if].astype(BF16)),
        'w_if': w_if,
        'w_ift': jnp.swapaxes(w_if, 1, 2)[:, :2 * nh, :],
        'b_if_row': jnp.pad(b_if, ((0, 0), (0, LANES - 2 * nh))).reshape(depth, 1, LANES),
        'b_if_col': jnp.broadcast_to(b_if[:, :, None], (depth, 2 * nh, LANES)),
        'mnorm': _rows(mlstm_norm),
        'w_pa': _pack_weights(w_proj_a),
        'w_u': _pack_weights(w_in[:, :, o_u:o_g].astype(BF16)),
        'conv_w': conv_dw.astype(F32),
        'conv_b': _rows(conv_b),
        'ln_g': _rows(conv_ln_g),
        'ln_b': _rows(conv_ln_b),
        'w_pb': _pack_weights(w_proj_b),
        'b_pb': _rows(b_proj_b),
        'w_g': _pack_weights(w_in[:, :, o_g:].astype(BF16)),
        'b_merge': _rows(b_merge),
        'w_out': _pack_weights(w_out),
        'g_post': _rows(norm_mix_post),
    }
    ffn = {
        'g_pre': _rows(norm_ffn_pre),
        'w_up': _pack_weights(w_up),
        'dw': ffn_dw.astype(F32),
        'dw_b': _rows(ffn_dw_b),
        'w_down': _pack_weights(w_down),
        'g_post': _rows(norm_ffn_post),
    }
    return mixer, ffn


def kernel(x_prompt, x_sample, state_mlstm_C, state_mlstm_n, state_mlstm_m, cache_conv, cache_ffn_conv, norm_mix_pre, norm_mix_post, norm_ffn_pre, norm_ffn_post, w_in, b_i, b_f, mlstm_norm, w_proj_a, conv_dw, conv_b, conv_ln_g, conv_ln_b, w_proj_b, b_proj_b, b_merge, w_out, w_up, ffn_dw, ffn_dw_b, w_down):
    assert b_i.shape[1] == NUM_HEADS
    depth = w_in.shape[0]
    mixer_p, ffn_p = _param_stacks(norm_mix_pre, norm_mix_post, norm_ffn_pre, norm_ffn_post, w_in, b_i, b_f,
                                   mlstm_norm, w_proj_a, conv_dw, conv_b, conv_ln_g, conv_ln_b, w_proj_b,
                                   b_proj_b, b_merge, w_out, w_up, ffn_dw, ffn_dw_b, w_down)
    m0 = jnp.broadcast_to(state_mlstm_m[..., None], state_mlstm_m.shape + (LANES,)).astype(F32)
    sample_init = (state_mlstm_C, state_mlstm_n, m0, cache_conv)
    yp, ys = x_prompt, x_sample
    pc = jnp.zeros((depth, x_prompt.shape[0]) + state_mlstm_C.shape[2:], F32)
    sc = jnp.zeros(state_mlstm_C.shape, F32)
    prompt_states, sample_states = [], []
    for l in range(depth):
        yp, pc, n1, m1, cb1 = _mixer_layer(yp, None, mixer_p, l, pc)
        yp, fb1 = _ffn_layer(yp, None, ffn_p, l)
        prompt_states.append((n1, m1, cb1, fb1))
        ys, sc, n2, m2, cb2 = _mixer_layer(ys, sample_init, mixer_p, l, sc)
        ys, fb2 = _ffn_layer(ys, cache_ffn_conv, ffn_p, l)
        sample_states.append((n2, m2, cb2, fb2))
    pn, pm, pcb, pfb = (jnp.stack(s) for s in zip(*prompt_states))
    sn, sm, scb, sfb = (jnp.stack(s) for s in zip(*sample_states))
    return (yp, ys, pc, pn, pm[..., 0], pcb, pfb, sc, sn, sm[..., 0], scb, sfb)
```

```python
import functools
import math

import jax
import jax.numpy as jnp
from jax import lax
from jax.experimental import pallas as pl
from jax.experimental.pallas import tpu as pltpu

EPS = 1e-6
LOG2_E = 1.0 / math.log(2.0)
NUM_HEADS = 4
SUBLANES = 8
LANES = 128
CONV_HIST_ROWS = 32
FFN_HIST_ROWS = 8
BLOCK_POSITIONS = 256
MIXER_BLOCK_ROWS = 256
FFN_BLOCK_ROWS = 256
CONV_PART_ROWS = 128
VMEM_LIMIT_BYTES = 56 * 1024 * 1024
STATE_VMEM_BYTES = 12 * 1024 * 1024
PACK_BLOCK_BYTES = 6 * 1024 * 1024

F32 = jnp.float32
BF16 = jnp.bfloat16


def _dot(a, b):
    return jnp.dot(a, b, preferred_element_type=F32)


def _wdot(a, w_ref, rows=slice(None), cols=slice(None)):
    return _dot(a, pltpu.bitcast(w_ref[rows, cols], BF16))


def _dot_nt(a, b):
    return lax.dot_general(a, b, (((1,), (1,)), ((), ())), preferred_element_type=F32)


def _dot_tn(a, b):
    return lax.dot_general(a, b, (((0,), (0,)), ((), ())), preferred_element_type=F32)


def _sigmoid(x):
    return 1.0 / (1.0 + jnp.exp2(x * (-LOG2_E)))


def _log_sigmoid(x):
    return jnp.minimum(x, 0.0) - jnp.log1p(jnp.exp(-jnp.abs(x)))


def _rms_scale(x):
    return x * lax.rsqrt(jnp.mean(x * x, axis=-1, keepdims=True) + EPS)


def _split_bf16(x):
    hi = x.astype(BF16)
    lo = (x - hi.astype(F32)).astype(BF16)
    return hi, lo


def _mlstm_gates(xb, w_if_ref, w_ift_ref, b_if_row_ref, b_if_col_ref, tril, triu):
    g_col = _dot(xb, w_if_ref[...]) + b_if_row_ref[...]
    hi, lo = _split_bf16(_log_sigmoid(g_col))
    bcum_col = _dot(tril, hi) + _dot(tril, lo)
    rows_used = w_ift_ref.shape[0]
    if xb.shape[0] % LANES == 0:
        g_row = jnp.transpose(g_col)[:rows_used, :]
        bcum_row = jnp.transpose(bcum_col)[:rows_used, :]
    else:
        g_row = _dot_nt(w_ift_ref[...], xb) + b_if_col_ref[:, 0:1]
        hi, lo = _split_bf16(_log_sigmoid(g_row))
        bcum_row = _dot(hi, triu) + _dot(lo, triu)
    return g_col, bcum_col, g_row, bcum_row


def _mlstm_block(q, k, v, gates, h, causal, c_prev, n_prev, m_prev):
    tt = q.shape[0]
    g_col, bcum_col, g_row, bcum_row = gates
    ig_c = g_col[:, h:h + 1]
    bc = bcum_col[:, NUM_HEADS + h:NUM_HEADS + h + 1]
    ig_r = g_row[h:h + 1, :]
    br = bcum_row[NUM_HEADS + h:NUM_HEADS + h + 1, :]
    qb, kb, vb = q.astype(BF16), k.astype(BF16), v.astype(BF16)

    logd = jnp.where(causal, (bc - br) + ig_r, -jnp.inf)
    g = bc + m_prev
    m_tok = jnp.maximum(g, jnp.max(logd, axis=-1, keepdims=True))
    w = jnp.exp(logd - m_tok)
    inter = jnp.exp(g - m_tok)
    s = _dot_nt(qb, kb) * w
    num = _dot(s.astype(BF16), vb) + inter * _dot(qb, c_prev.astype(BF16))
    den = jnp.sum(s, axis=-1, keepdims=True) + inter * jnp.sum(q * n_prev, axis=-1, keepdims=True)
    hh = num * (1.0 / jnp.maximum(jnp.abs(den), jnp.exp(-m_tok)))

    m_new = m_tok[tt - 1:tt, :]
    decay = inter[tt - 1:tt, :]
    w_last = jnp.exp((bc[tt - 1:tt, :] - bc) + ig_c - m_new)
    c_new = decay * c_prev + _dot_tn(kb, (w_last * v).astype(BF16))
    n_new = decay * n_prev + jnp.sum(w_last * k, axis=0, keepdims=True)
    return hh, c_new, n_new, m_new


def _causal_conv_tile(glu_scr, conv_scr, conv_w_ref, conv_b_ref, b, tt, ls, hist0):
    conv_w = conv_w_ref.shape[0]
    part = min(tt, CONV_PART_ROWS)
    span = part + CONV_HIST_ROWS
    for p0 in range(0, tt, part):
        acc = jnp.broadcast_to(conv_b_ref[:, ls], (part, LANES))
        rows_in = glu_scr[b, p0:p0 + span, ls]
        for r in range(SUBLANES):
            offs = [o for o in range(hist0, hist0 + conv_w) if o % SUBLANES == r]
            window = rows_in if r == 0 else pltpu.roll(rows_in, span - r, axis=0)
            for o in offs:
                kk = o - hist0
                acc = acc + conv_w_ref[kk:kk + 1, ls] * window[o - r:o - r + part]
        conv_scr[b * tt + p0:b * tt + p0 + part, ls] = acc


def _mixer_kernel(*refs, nb, tt, has_init):
    if has_init:
        (x_ref, c0_ref, n0_ref, m0_ref, cb0_ref, *rest) = refs
    else:
        (x_ref, *rest) = refs
    (g_pre_ref, w_qkvo_ref, w_if_ref, w_ift_ref, b_if_row_ref, b_if_col_ref, mnorm_ref, w_pa_ref,
     w_u_ref, conv_w_ref, conv_b_ref, ln_g_ref, ln_b_ref, w_pb_ref, b_pb_ref, w_g_ref, b_merge_ref,
     w_out_ref, g_post_ref, _memory_stack_ref,
     y_ref, c_ref, n_ref, m_ref, cb_ref,
     xn_scr, hg_scr, glu_scr, conv_scr, gt_scr, gcol_scr, grow_scr) = rest

    t = pl.program_id(1)
    rows = nb * tt
    d_model = x_ref.shape[-1]
    d_a = mnorm_ref.shape[1]
    dh = d_a // NUM_HEADS
    d_b = conv_b_ref.shape[1]
    conv_w = conv_w_ref.shape[0]
    hist0 = CONV_HIST_ROWS - (conv_w - 1)

    @pl.when(t == 0)
    def _init_state():
        glu_scr[:, 0:CONV_HIST_ROWS, :] = jnp.zeros((nb, CONV_HIST_ROWS, d_b), F32)
        if has_init:
            c_ref[...] = c0_ref[...]
            n_ref[...] = n0_ref[...]
            m_ref[...] = m0_ref[...]
            glu_scr[:, hist0:CONV_HIST_ROWS, :] = cb0_ref[...]
        else:
            c_ref[...] = jnp.zeros(c_ref.shape, F32)
            n_ref[...] = jnp.zeros(n_ref.shape, F32)
            m_ref[...] = jnp.zeros(m_ref.shape, F32)

    x = x_ref[...].reshape(rows, d_model)
    xn_scr[...] = (_rms_scale(x) * g_pre_ref[...]).astype(BF16)

    row_id = lax.broadcasted_iota(jnp.int32, (tt, tt), 0)
    col_id = lax.broadcasted_iota(jnp.int32, (tt, tt), 1)
    tril = jnp.where(row_id >= col_id, 1.0, 0.0).astype(BF16)
    triu = jnp.where(col_id >= row_id, 1.0, 0.0).astype(BF16)
    for b in range(nb):
        g_col, bcum_col, g_row, bcum_row = _mlstm_gates(
            xn_scr[b * tt:(b + 1) * tt, :], w_if_ref, w_ift_ref, b_if_row_ref, b_if_col_ref, tril, triu)
        gcol_scr[b, 0], gcol_scr[b, 1] = g_col, bcum_col
        grow_scr[b, 0], grow_scr[b, 1] = g_row, bcum_row

    xn = xn_scr[...]
    u = _wdot(xn, w_u_ref)
    glu_scr[:, CONV_HIST_ROWS:CONV_HIST_ROWS + tt, :] = (u[:, :d_b] * _sigmoid(u[:, d_b:])).reshape(nb, tt, d_b)

    dg = d_model // NUM_HEADS
    causal = row_id >= col_id
    lane_tiles = d_b // LANES
    for h in range(NUM_HEADS):
        for j in range(h * lane_tiles // NUM_HEADS, (h + 1) * lane_tiles // NUM_HEADS):
            for b in range(nb):
                _causal_conv_tile(glu_scr, conv_scr, conv_w_ref, conv_b_ref, b, tt,
                                  slice(j * LANES, (j + 1) * LANES), hist0)
        hs = slice(h * dh, (h + 1) * dh)
        q_all = _wdot(xn, w_qkvo_ref, cols=slice(h * dh, (h + 1) * dh))
        k_all = _wdot(xn, w_qkvo_ref, cols=slice(d_a + h * dh, d_a + (h + 1) * dh)) * (1.0 / math.sqrt(dh))
        v_all = _wdot(xn, w_qkvo_ref, cols=slice(2 * d_a + h * dh, 2 * d_a + (h + 1) * dh))
        o_all = _wdot(xn, w_qkvo_ref, cols=slice(3 * d_a + h * dh, 3 * d_a + (h + 1) * dh))
        for b in range(nb):
            rs = slice(b * tt, (b + 1) * tt)
            gates = (gcol_scr[b, 0], gcol_scr[b, 1], grow_scr[b, 0], grow_scr[b, 1])
            hh, c_new, n_new, m_new = _mlstm_block(
                q_all[rs], k_all[rs], v_all[rs], gates, h, causal,
                c_ref[b, h], n_ref[b, h:h + 1, :], m_ref[b, h:h + 1, 0:1])
            c_ref[b, h] = c_new
            n_ref[b, h:h + 1, :] = n_new
            m_ref[b, h:h + 1, :] = jnp.broadcast_to(m_new, (1, LANES))
            hn = _rms_scale(hh) * mnorm_ref[:, hs]
            hg_scr[rs, hs] = (hn * _sigmoid(o_all[rs])).astype(BF16)

        for half in range(2):
            gs = slice(half * d_model + h * dg, half * d_model + (h + 1) * dg)
            gt_scr[:, gs] = _sigmoid(_wdot(xn, w_g_ref, cols=gs) + b_merge_ref[:, gs])
    glu_scr[:, 0:CONV_HIST_ROWS, :] = glu_scr[:, tt:tt + CONV_HIST_ROWS, :]

    if nb % 2 == 0:
        splits = [(slice(i * nb // 2, (i + 1) * nb // 2), slice(0, tt)) for i in range(2)]
    elif nb == 1 and tt % (4 * SUBLANES) == 0:
        splits = [(slice(0, 1), slice(i * tt // 2, (i + 1) * tt // 2)) for i in range(2)]
    else:
        splits = [(slice(0, nb), slice(0, tt))]
    part_rows = rows // len(splits)
    for i, (bs, ts) in enumerate(splits):
        rs = slice(i * part_rows, (i + 1) * part_rows)
        c = conv_scr[rs, :]
        xc = c - jnp.mean(c, axis=-1, keepdims=True)
        cn = xc * lax.rsqrt(jnp.mean(xc * xc, axis=-1, keepdims=True) + EPS) * ln_g_ref[...] + ln_b_ref[...]
        act = cn * _sigmoid(cn)
        yb = _wdot(act.astype(BF16), w_pb_ref) + b_pb_ref[...]
        ya = _wdot(hg_scr[rs, :], w_pa_ref)
        mix = gt_scr[rs, :d_model] * ya + gt_scr[rs, d_model:] * yb
        mo = _wdot(mix.astype(BF16), w_out_ref)
        y = x_ref[bs, ts, :].reshape(part_rows, d_model) + _rms_scale(mo) * g_post_ref[...]
        y_ref[bs, ts, :] = y.reshape(bs.stop - bs.start, ts.stop - ts.start, d_model)

    @pl.when(t == pl.num_programs(1) - 1)
    def _emit_conv_state():
        cb_ref[...] = glu_scr[:, hist0:CONV_HIST_ROWS, :]


def _ffn_kernel(*refs, nb, tt, has_init):
    if has_init:
        (x_ref, fb0_ref, *rest) = refs
    else:
        (x_ref, *rest) = refs
    (g_pre_ref, w_up_ref, dw_ref, dw_b_ref, w_down_ref, g_post_ref,
     y_ref, fb_ref, hist_scr) = rest

    t = pl.program_id(1)
    rows = nb * tt
    d_model = x_ref.shape[-1]
    d_ff = dw_ref.shape[1] // 2
    width = dw_ref.shape[0]
    hist0 = FFN_HIST_ROWS - (width - 1)

    @pl.when(t == 0)
    def _init_state():
        hist_scr[...] = jnp.zeros(hist_scr.shape, F32)
        if has_init:
            hist_scr[:, hist0:FFN_HIST_ROWS, :] = fb0_ref[...]

    x = x_ref[...].reshape(rows, d_model)
    hn = (_rms_scale(x) * g_pre_ref[...]).astype(BF16)
    up = _wdot(hn, w_up_ref)

    sub_row = lax.broadcasted_iota(jnp.int32, (SUBLANES, 2 * d_ff), 0)
    conv = dw_ref[width - 1:width, :] * up + dw_b_ref[...]
    for kk in range(width - 1):
        shift = width - 1 - kk
        moved = pltpu.roll(up, shift, axis=0)
        pieces = []
        for b in range(nb):
            head = jnp.where(sub_row < shift, pltpu.roll(hist_scr[b], shift, axis=0),
                             moved[b * tt:b * tt + SUBLANES])
            pieces += [head, moved[b * tt + SUBLANES:(b + 1) * tt]]
        conv = conv + dw_ref[kk:kk + 1, :] * jnp.concatenate(pieces, axis=0)
    for b in range(nb):
        hist_scr[b] = up[(b + 1) * tt - FFN_HIST_ROWS:(b + 1) * tt]

    gate = conv[:, :d_ff]
    lin = -2.0 * LOG2_E * math.sqrt(2.0 / math.pi)
    cdf = 1.0 / (1.0 + jnp.exp2(gate * (lin + (lin * 0.044715) * (gate * gate))))
    f = _wdot((gate * conv[:, d_ff:] * cdf).astype(BF16), w_down_ref)
    y = x_ref[...].reshape(rows, d_model) + _rms_scale(f) * g_post_ref[...]
    y_ref[...] = y.reshape(nb, tt, d_model)

    @pl.when(t == pl.num_programs(1) - 1)
    def _emit_conv_state():
        fb_ref[...] = hist_scr[:, hist0:FFN_HIST_ROWS, :]


def _block_rows(batch, seq, max_rows, bytes_per_batch_row=0):
    tt = BLOCK_POSITIONS if seq % BLOCK_POSITIONS == 0 else seq
    assert tt % (2 * SUBLANES) == 0 and tt >= CONV_HIST_ROWS, (batch, seq)
    nb = max(1, min(batch, max_rows // tt))
    if bytes_per_batch_row:
        nb = max(1, min(nb, STATE_VMEM_BYTES // bytes_per_batch_row))
    while batch % nb:
        nb -= 1
    return nb, tt


def _layer_spec(stacked, layer):
    zeros = (0,) * (stacked.ndim - 1)
    return pl.BlockSpec((None,) + stacked.shape[1:], lambda b, t: (layer,) + zeros, pipeline_mode=pl.Buffered(1))


def _state_spec(stacked_shape, layer, nb, buffers=None):
    zeros = (0,) * (len(stacked_shape) - 2)
    mode = None if buffers is None else pl.Buffered(buffers)
    return pl.BlockSpec((None, nb) + tuple(stacked_shape[2:]), lambda b, t: (layer, b) + zeros, pipeline_mode=mode)


def _per_batch(shape, nb):
    return pl.BlockSpec((nb,) + tuple(shape[1:]), lambda b, t: (b,) + (0,) * (len(shape) - 1))


MIXER_PARAMS = ('g_pre', 'w_qkvo', 'w_if', 'w_ift', 'b_if_row', 'b_if_col', 'mnorm', 'w_pa', 'w_u', 'conv_w',
                'conv_b', 'ln_g', 'ln_b', 'w_pb', 'b_pb', 'w_g', 'b_merge', 'w_out', 'g_post')
FFN_PARAMS = ('g_pre', 'w_up', 'dw', 'dw_b', 'w_down', 'g_post')


def _mixer_layer(x, init, p, layer, memory_stack):
    batch, seq, d_model = x.shape
    d_a = p['mnorm'].shape[-1]
    dh = d_a // NUM_HEADS
    d_b = p['conv_b'].shape[-1]
    conv_w = p['conv_w'].shape[-2]
    has_init = init is not None
    memory_bytes = NUM_HEADS * dh * dh * 4 * (3 if has_init else 2)
    nb, tt = _block_rows(batch, seq, MIXER_BLOCK_ROWS, memory_bytes)
    x_spec = pl.BlockSpec((nb, tt, d_model), lambda b, t: (b, t, 0))
    weights = [p[k] for k in MIXER_PARAMS]
    state_shapes = [(batch, NUM_HEADS, dh, dh), (batch, NUM_HEADS, dh), (batch, NUM_HEADS, LANES),
                    (batch, conv_w - 1, d_b)]
    operands = [x] + (list(init) if has_init else []) + weights + [memory_stack]
    init_specs = [_state_spec(s.shape, layer, nb, buffers=1 if i == 0 else None) for i, s in enumerate(init or ())]
    in_specs = ([x_spec] + init_specs + [_layer_spec(w, layer) for w in weights]
                + [pl.BlockSpec(memory_space=pl.ANY)])
    out_shape = ([jax.ShapeDtypeStruct(x.shape, F32), jax.ShapeDtypeStruct(memory_stack.shape, F32)]
                 + [jax.ShapeDtypeStruct(s, F32) for s in state_shapes[1:]])
    out_specs = ([x_spec, _state_spec(memory_stack.shape, layer, nb)]
                 + [_per_batch(s, nb) for s in state_shapes[1:]])
    rows = nb * tt
    return pl.pallas_call(
        functools.partial(_mixer_kernel, nb=nb, tt=tt, has_init=has_init),
        grid=(batch // nb, seq // tt),
        in_specs=in_specs, out_specs=out_specs, out_shape=out_shape,
        input_output_aliases={len(operands) - 1: 1},
        scratch_shapes=[pltpu.VMEM((rows, d_model), BF16), pltpu.VMEM((rows, d_a), BF16),
                        pltpu.VMEM((nb, CONV_HIST_ROWS + tt, d_b), F32),
                        pltpu.VMEM((rows, d_b), F32), pltpu.VMEM((rows, 2 * d_model), F32),
                        pltpu.VMEM((nb, 2, tt, LANES), F32),
                        pltpu.VMEM((nb, 2, SUBLANES, tt), F32)],
        compiler_params=pltpu.CompilerParams(dimension_semantics=("arbitrary", "arbitrary"),
                                             vmem_limit_bytes=VMEM_LIMIT_BYTES),
        name="mixer_init" if has_init else "mixer",
    )(*operands)


def _ffn_layer(x, init, p, layer):
    batch, seq, d_model = x.shape
    nb, tt = _block_rows(batch, seq, FFN_BLOCK_ROWS)
    d_ff = p['dw'].shape[-1] // 2
    width = p['dw'].shape[-2]
    has_init = init is not None
    x_spec = pl.BlockSpec((nb, tt, d_model), lambda b, t: (b, t, 0))
    weights = [p[k] for k in FFN_PARAMS]
    fb_shape = (batch, width - 1, 2 * d_ff)
    operands = [x] + ([init] if has_init else []) + weights
    in_specs = ([x_spec] + ([_state_spec(init.shape, layer, nb)] if has_init else [])
                + [_layer_spec(w, layer) for w in weights])
    return pl.pallas_call(
        functools.partial(_ffn_kernel, nb=nb, tt=tt, has_init=has_init),
        grid=(batch // nb, seq // tt),
        in_specs=in_specs,
        out_specs=[x_spec, _per_batch(fb_shape, nb)],
        out_shape=[jax.ShapeDtypeStruct(x.shape, F32), jax.ShapeDtypeStruct(fb_shape, F32)],
        scratch_shapes=[pltpu.VMEM((nb, FFN_HIST_ROWS, 2 * d_ff), F32)],
        compiler_params=pltpu.CompilerParams(dimension_semantics=("arbitrary", "arbitrary"),
                                             vmem_limit_bytes=VMEM_LIMIT_BYTES),
        name="ffn_init" if has_init else "ffn",
    )(*operands)


def _pack_kernel(w_ref, o_ref):
    o_ref[...] = pltpu.bitcast(w_ref[...].astype(BF16), jnp.uint32)


def _pack_weights(w):
    depth, k, n = w.shape
    tn = max(c for c in range(LANES, n + 1, LANES) if n % c == 0 and k * c * w.dtype.itemsize <= PACK_BLOCK_BYTES)
    return pl.pallas_call(
        _pack_kernel,
        grid=(depth, n // tn),
        in_specs=[pl.BlockSpec((None, k, tn), lambda l, j: (l, 0, j))],
        out_specs=pl.BlockSpec((None, k // 2, tn), lambda l, j: (l, 0, j)),
        out_shape=jax.ShapeDtypeStruct((depth, k // 2, n), jnp.uint32),
        name="pack_weights",
    )(w)


def _rows(v):
    return v.reshape(v.shape[0], 1, v.shape[1]).astype(F32)


def _param_stacks(norm_mix_pre, norm_mix_post, norm_ffn_pre, norm_ffn_post, w_in, b_i, b_f, mlstm_norm, w_proj_a,
                  conv_dw, conv_b, conv_ln_g, conv_ln_b, w_proj_b, b_proj_b, b_merge, w_out, w_up, ffn_dw,
                  ffn_dw_b, w_down):
    depth = w_in.shape[0]
    d_a = w_proj_a.shape[1]
    d_b = w_proj_b.shape[1]
    nh = b_i.shape[1]
    o_if = 4 * d_a
    o_u = o_if + 2 * nh
    o_g = o_u + 2 * d_b
    w_if = w_in[:, :, o_if:o_if + LANES].astype(BF16)
    b_if = jnp.concatenate([b_i, b_f], axis=1).astype(F32)
    mixer = {
        'g_pre': _rows(norm_mix_pre),
        'w_qkvo': _pack_weights(w_in[:, :, :o_if].astype(BF16)),
        'w_if': w_if,
        'w_ift': jnp.swapaxes(w_if, 1, 2)[:, :2 * nh, :],
        'b_if_row': jnp.pad(b_if, ((0, 0), (0, LANES - 2 * nh))).reshape(depth, 1, LANES),
        'b_if_col': jnp.broadcast_to(b_if[:, :, None], (depth, 2 * nh, LANES)),
        'mnorm': _rows(mlstm_norm),
        'w_pa': _pack_weights(w_proj_a),
        'w_u': _pack_weights(w_in[:, :, o_u:o_g].astype(BF16)),
        'conv_w': conv_dw.astype(F32),
        'conv_b': _rows(conv_b),
        'ln_g': _rows(conv_ln_g),
        'ln_b': _rows(conv_ln_b),
        'w_pb': _pack_weights(w_proj_b),
        'b_pb': _rows(b_proj_b),
        'w_g': _pack_weights(w_in[:, :, o_g:].astype(BF16)),
        'b_merge': _rows(b_merge),
        'w_out': _pack_weights(w_out),
        'g_post': _rows(norm_mix_post),
    }
    ffn = {
        'g_pre': _rows(norm_ffn_pre),
        'w_up': _pack_weights(w_up),
        'dw': ffn_dw.astype(F32),
        'dw_b': _rows(ffn_dw_b),
        'w_down': _pack_weights(w_down),
        'g_post': _rows(norm_ffn_post),
    }
    return mixer, ffn


def kernel(x_prompt, x_sample, state_mlstm_C, state_mlstm_n, state_mlstm_m, cache_conv, cache_ffn_conv, norm_mix_pre, norm_mix_post, norm_ffn_pre, norm_ffn_post, w_in, b_i, b_f, mlstm_norm, w_proj_a, conv_dw, conv_b, conv_ln_g, conv_ln_b, w_proj_b, b_proj_b, b_merge, w_out, w_up, ffn_dw, ffn_dw_b, w_down):
    assert b_i.shape[1] == NUM_HEADS
    depth = w_in.shape[0]
    mixer_p, ffn_p = _param_stacks(norm_mix_pre, norm_mix_post, norm_ffn_pre, norm_ffn_post, w_in, b_i, b_f,
                                   mlstm_norm, w_proj_a, conv_dw, conv_b, conv_ln_g, conv_ln_b, w_proj_b,
                                   b_proj_b, b_merge, w_out, w_up, ffn_dw, ffn_dw_b, w_down)
    m0 = jnp.broadcast_to(state_mlstm_m[..., None], state_mlstm_m.shape + (LANES,)).astype(F32)
    sample_init = (state_mlstm_C, state_mlstm_n, m0, cache_conv)
    yp, ys = x_prompt, x_sample
    pc = jnp.zeros((depth, x_prompt.shape[0]) + state_mlstm_C.shape[2:], F32)
    sc = jnp.zeros(state_mlstm_C.shape, F32)
    prompt_states, sample_states = [], []
    for l in range(depth):
        yp, pc, n1, m1, cb1 = _mixer_layer(yp, None, mixer_p, l, pc)
        yp, fb1 = _ffn_layer(yp, None, ffn_p, l)
        prompt_states.append((n1, m1, cb1, fb1))
        ys, sc, n2, m2, cb2 = _mixer_layer(ys, sample_init, mixer_p, l, sc)
        ys, fb2 = _ffn_layer(ys, cache_ffn_conv, ffn_p, l)
        sample_states.append((n2, m2, cb2, fb2))
    pn, pm, pcb, pfb = (jnp.stack(s) for s in zip(*prompt_states))
    sn, sm, scb, sfb = (jnp.stack(s) for s in zip(*sample_states))
    return (yp, ys, pc, pn, pm[..., 0], pcb, pfb, sc, sn, sm[..., 0], scb, sfb)
```

```python
import functools
import math

import jax
import jax.numpy as jnp
from jax import lax
from jax.experimental import pallas as pl
from jax.experimental.pallas import tpu as pltpu

EPS = 1e-6
LOG2_E = 1.0 / math.log(2.0)
NUM_HEADS = 4
SUBLANES = 8
LANES = 128
CONV_HIST_ROWS = 32
FFN_HIST_ROWS = 8
BLOCK_POSITIONS = 256
MIXER_BLOCK_ROWS = 256
FFN_BLOCK_ROWS = 512
CONV_PART_ROWS = 128
VMEM_LIMIT_BYTES = 56 * 1024 * 1024
STATE_VMEM_BYTES = 8 * 1024 * 1024
PACK_BLOCK_BYTES = 6 * 1024 * 1024

F32 = jnp.float32
BF16 = jnp.bfloat16


def _dot(a, b):
    return jnp.dot(a, b, preferred_element_type=F32)


def _wdot(a, w_ref, rows=slice(None), cols=slice(None)):
    return _dot(a, pltpu.bitcast(w_ref[rows, cols], BF16))


def _dot_nt(a, b):
    return lax.dot_general(a, b, (((1,), (1,)), ((), ())), preferred_element_type=F32)


def _dot_tn(a, b):
    return lax.dot_general(a, b, (((0,), (0,)), ((), ())), preferred_element_type=F32)


def _sigmoid(x):
    return 1.0 / (1.0 + jnp.exp2(x * (-LOG2_E)))


def _log_sigmoid(x):
    return jnp.minimum(x, 0.0) - jnp.log1p(jnp.exp(-jnp.abs(x)))


def _rms_scale(x):
    return x * lax.rsqrt(jnp.mean(x * x, axis=-1, keepdims=True) + EPS)


def _split_bf16(x):
    hi = x.astype(BF16)
    lo = (x - hi.astype(F32)).astype(BF16)
    return hi, lo


def _mlstm_gates(xb, w_if_ref, w_ift_ref, b_if_row_ref, b_if_col_ref, tril, triu):
    g_col = _dot(xb, w_if_ref[...]) + b_if_row_ref[...]
    hi, lo = _split_bf16(_log_sigmoid(g_col))
    bcum_col = _dot(tril, hi) + _dot(tril, lo)
    rows_used = w_ift_ref.shape[0]
    if xb.shape[0] % LANES == 0:
        g_row = jnp.transpose(g_col)[:rows_used, :]
        bcum_row = jnp.transpose(bcum_col)[:rows_used, :]
    else:
        g_row = _dot_nt(w_ift_ref[...], xb) + b_if_col_ref[:, 0:1]
        hi, lo = _split_bf16(_log_sigmoid(g_row))
        bcum_row = _dot(hi, triu) + _dot(lo, triu)
    return g_col, bcum_col, g_row, bcum_row


def _mlstm_block(q, k, v, gates, h, causal, c_prev, n_prev, m_prev):
    tt = q.shape[0]
    g_col, bcum_col, g_row, bcum_row = gates
    ig_c = g_col[:, h:h + 1]
    bc = bcum_col[:, NUM_HEADS + h:NUM_HEADS + h + 1]
    ig_r = g_row[h:h + 1, :]
    br = bcum_row[NUM_HEADS + h:NUM_HEADS + h + 1, :]
    qb, kb, vb = q.astype(BF16), k.astype(BF16), v.astype(BF16)

    logd = jnp.where(causal, (bc - br) + ig_r, -jnp.inf)
    g = bc + m_prev
    m_tok = jnp.maximum(g, jnp.max(logd, axis=-1, keepdims=True))
    w = jnp.exp(logd - m_tok)
    inter = jnp.exp(g - m_tok)
    s = _dot_nt(qb, kb) * w
    num = _dot(s.astype(BF16), vb) + inter * _dot(qb, c_prev.astype(BF16))
    den = jnp.sum(s, axis=-1, keepdims=True) + inter * jnp.sum(q * n_prev, axis=-1, keepdims=True)
    hh = num * (1.0 / jnp.maximum(jnp.abs(den), jnp.exp(-m_tok)))

    m_new = m_tok[tt - 1:tt, :]
    decay = inter[tt - 1:tt, :]
    w_last = jnp.exp((bc[tt - 1:tt, :] - bc) + ig_c - m_new)
    c_new = decay * c_prev + _dot_tn(kb, (w_last * v).astype(BF16))
    n_new = decay * n_prev + jnp.sum(w_last * k, axis=0, keepdims=True)
    return hh, c_new, n_new, m_new


def _causal_conv_tile(glu_scr, conv_scr, conv_w_ref, conv_b_ref, b, tt, ls, hist0):
    conv_w = conv_w_ref.shape[0]
    part = min(tt, CONV_PART_ROWS)
    span = part + CONV_HIST_ROWS
    for p0 in range(0, tt, part):
        acc = jnp.broadcast_to(conv_b_ref[:, ls], (part, LANES))
        rows_in = glu_scr[b, p0:p0 + span, ls]
        for r in range(SUBLANES):
            offs = [o for o in range(hist0, hist0 + conv_w) if o % SUBLANES == r]
            window = rows_in if r == 0 else pltpu.roll(rows_in, span - r, axis=0)
            for o in offs:
                kk = o - hist0
                acc = acc + conv_w_ref[kk:kk + 1, ls] * window[o - r:o - r + part]
        conv_scr[b * tt + p0:b * tt + p0 + part, ls] = acc


def _mixer_kernel(*refs, nb, tt, has_init):
    if has_init:
        (x_ref, c0_ref, n0_ref, m0_ref, cb0_ref, *rest) = refs
    else:
        (x_ref, *rest) = refs
    (g_pre_ref, w_qkvo_ref, w_if_ref, w_ift_ref, b_if_row_ref, b_if_col_ref, mnorm_ref, w_pa_ref,
     w_u_ref, conv_w_ref, conv_b_ref, ln_g_ref, ln_b_ref, w_pb_ref, b_pb_ref, w_g_ref, b_merge_ref,
     w_out_ref, g_post_ref, _memory_stack_ref,
     y_ref, c_ref, n_ref, m_ref, cb_ref,
     xn_scr, hg_scr, glu_scr, conv_scr, gt_scr, gcol_scr, grow_scr) = rest

    t = pl.program_id(1)
    rows = nb * tt
    d_model = x_ref.shape[-1]
    d_a = mnorm_ref.shape[1]
    dh = d_a // NUM_HEADS
    d_b = conv_b_ref.shape[1]
    conv_w = conv_w_ref.shape[0]
    hist0 = CONV_HIST_ROWS - (conv_w - 1)

    @pl.when(t == 0)
    def _init_state():
        glu_scr[:, 0:CONV_HIST_ROWS, :] = jnp.zeros((nb, CONV_HIST_ROWS, d_b), F32)
        if has_init:
            c_ref[...] = c0_ref[...]
            n_ref[...] = n0_ref[...]
            m_ref[...] = m0_ref[...]
            glu_scr[:, hist0:CONV_HIST_ROWS, :] = cb0_ref[...]
        else:
            c_ref[...] = jnp.zeros(c_ref.shape, F32)
            n_ref[...] = jnp.zeros(n_ref.shape, F32)
            m_ref[...] = jnp.zeros(m_ref.shape, F32)

    x = x_ref[...].reshape(rows, d_model)
    xn_scr[...] = (_rms_scale(x) * g_pre_ref[...]).astype(BF16)

    row_id = lax.broadcasted_iota(jnp.int32, (tt, tt), 0)
    col_id = lax.broadcasted_iota(jnp.int32, (tt, tt), 1)
    tril = jnp.where(row_id >= col_id, 1.0, 0.0).astype(BF16)
    triu = jnp.where(col_id >= row_id, 1.0, 0.0).astype(BF16)
    for b in range(nb):
        g_col, bcum_col, g_row, bcum_row = _mlstm_gates(
            xn_scr[b * tt:(b + 1) * tt, :], w_if_ref, w_ift_ref, b_if_row_ref, b_if_col_ref, tril, triu)
        gcol_scr[b, 0], gcol_scr[b, 1] = g_col, bcum_col
        grow_scr[b, 0], grow_scr[b, 1] = g_row, bcum_row

    xn = xn_scr[...]
    u = _wdot(xn, w_u_ref)
    glu_scr[:, CONV_HIST_ROWS:CONV_HIST_ROWS + tt, :] = (u[:, :d_b] * _sigmoid(u[:, d_b:])).reshape(nb, tt, d_b)

    dg = d_model // NUM_HEADS
    causal = row_id >= col_id
    lane_tiles = d_b // LANES
    for h in range(NUM_HEADS):
        for j in range(h * lane_tiles // NUM_HEADS, (h + 1) * lane_tiles // NUM_HEADS):
            for b in range(nb):
                _causal_conv_tile(glu_scr, conv_scr, conv_w_ref, conv_b_ref, b, tt,
                                  slice(j * LANES, (j + 1) * LANES), hist0)
        hs = slice(h * dh, (h + 1) * dh)
        q_all = _wdot(xn, w_qkvo_ref, cols=slice(h * dh, (h + 1) * dh))
        k_all = _wdot(xn, w_qkvo_ref, cols=slice(d_a + h * dh, d_a + (h + 1) * dh)) * (1.0 / math.sqrt(dh))
        v_all = _wdot(xn, w_qkvo_ref, cols=slice(2 * d_a + h * dh, 2 * d_a + (h + 1) * dh))
        o_all = _wdot(xn, w_qkvo_ref, cols=slice(3 * d_a + h * dh, 3 * d_a + (h + 1) * dh))
        for b in range(nb):
            rs = slice(b * tt, (b + 1) * tt)
            gates = (gcol_scr[b, 0], gcol_scr[b, 1], grow_scr[b, 0], grow_scr[b, 1])
            hh, c_new, n_new, m_new = _mlstm_block(
                q_all[rs], k_all[rs], v_all[rs], gates, h, causal,
                c_ref[b, h], n_ref[b, h:h + 1, :], m_ref[b, h:h + 1, 0:1])
            c_ref[b, h] = c_new
            n_ref[b, h:h + 1, :] = n_new
            m_ref[b, h:h + 1, :] = jnp.broadcast_to(m_new, (1, LANES))
            hn = _rms_scale(hh) * mnorm_ref[:, hs]
            hg_scr[rs, hs] = (hn * _sigmoid(o_all[rs])).astype(BF16)

        for half in range(2):
            gs = slice(half * d_model + h * dg, half * d_model + (h + 1) * dg)
            gt_scr[:, gs] = _sigmoid(_wdot(xn, w_g_ref, cols=gs) + b_merge_ref[:, gs])
    glu_scr[:, 0:CONV_HIST_ROWS, :] = glu_scr[:, tt:tt + CONV_HIST_ROWS, :]

    if nb % 2 == 0:
        splits = [(slice(i * nb // 2, (i + 1) * nb // 2), slice(0, tt)) for i in range(2)]
    elif nb == 1 and tt % (4 * SUBLANES) == 0:
        splits = [(slice(0, 1), slice(i * tt // 2, (i + 1) * tt // 2)) for i in range(2)]
    else:
        splits = [(slice(0, nb), slice(0, tt))]
    part_rows = rows // len(splits)
    for i, (bs, ts) in enumerate(splits):
        rs = slice(i * part_rows, (i + 1) * part_rows)
        c = conv_scr[rs, :]
        xc = c - jnp.mean(c, axis=-1, keepdims=True)
        cn = xc * lax.rsqrt(jnp.mean(xc * xc, axis=-1, keepdims=True) + EPS) * ln_g_ref[...] + ln_b_ref[...]
        act = cn * _sigmoid(cn)
        yb = _wdot(act.astype(BF16), w_pb_ref) + b_pb_ref[...]
        ya = _wdot(hg_scr[rs, :], w_pa_ref)
        mix = gt_scr[rs, :d_model] * ya + gt_scr[rs, d_model:] * yb
        mo = _wdot(mix.astype(BF16), w_out_ref)
        y = x_ref[bs, ts, :].reshape(part_rows, d_model) + _rms_scale(mo) * g_post_ref[...]
        y_ref[bs, ts, :] = y.reshape(bs.stop - bs.start, ts.stop - ts.start, d_model)

    @pl.when(t == pl.num_programs(1) - 1)
    def _emit_conv_state():
        cb_ref[...] = glu_scr[:, hist0:CONV_HIST_ROWS, :]


def _ffn_kernel(*refs, nb, tt, has_init):
    if has_init:
        (x_ref, fb0_ref, *rest) = refs
    else:
        (x_ref, *rest) = refs
    (g_pre_ref, w_up_ref, dw_ref, dw_b_ref, w_down_ref, g_post_ref,
     y_ref, fb_ref, hist_scr) = rest

    t = pl.program_id(1)
    rows = nb * tt
    d_model = x_ref.shape[-1]
    d_ff = dw_ref.shape[1] // 2
    width = dw_ref.shape[0]
    hist0 = FFN_HIST_ROWS - (width - 1)

    @pl.when(t == 0)
    def _init_state():
        hist_scr[...] = jnp.zeros(hist_scr.shape, F32)
        if has_init:
            hist_scr[:, hist0:FFN_HIST_ROWS, :] = fb0_ref[...]

    x = x_ref[...].reshape(rows, d_model)
    hn = (_rms_scale(x) * g_pre_ref[...]).astype(BF16)
    up = _wdot(hn, w_up_ref)

    sub_row = lax.broadcasted_iota(jnp.int32, (SUBLANES, 2 * d_ff), 0)
    conv = dw_ref[width - 1:width, :] * up + dw_b_ref[...]
    for kk in range(width - 1):
        shift = width - 1 - kk
        moved = pltpu.roll(up, shift, axis=0)
        pieces = []
        for b in range(nb):
            head = jnp.where(sub_row < shift, pltpu.roll(hist_scr[b], shift, axis=0),
                             moved[b * tt:b * tt + SUBLANES])
            pieces += [head, moved[b * tt + SUBLANES:(b + 1) * tt]]
        conv = conv + dw_ref[kk:kk + 1, :] * jnp.concatenate(pieces, axis=0)
    for b in range(nb):
        hist_scr[b] = up[(b + 1) * tt - FFN_HIST_ROWS:(b + 1) * tt]

    gate = conv[:, :d_ff]
    lin = -2.0 * LOG2_E * math.sqrt(2.0 / math.pi)
    cdf = 1.0 / (1.0 + jnp.exp2(gate * (lin + (lin * 0.044715) * (gate * gate))))
    f = _wdot((gate * conv[:, d_ff:] * cdf).astype(BF16), w_down_ref)
    y = x_ref[...].reshape(rows, d_model) + _rms_scale(f) * g_post_ref[...]
    y_ref[...] = y.reshape(nb, tt, d_model)

    @pl.when(t == pl.num_programs(1) - 1)
    def _emit_conv_state():
        fb_ref[...] = hist_scr[:, hist0:FFN_HIST_ROWS, :]


def _block_rows(batch, seq, max_rows, bytes_per_batch_row=0):
    tt = BLOCK_POSITIONS if seq % BLOCK_POSITIONS == 0 else seq
    assert tt % (2 * SUBLANES) == 0 and tt >= CONV_HIST_ROWS, (batch, seq)
    nb = max(1, min(batch, max_rows // tt))
    if bytes_per_batch_row:
        nb = max(1, min(nb, STATE_VMEM_BYTES // bytes_per_batch_row))
    while batch % nb:
        nb -= 1
    return nb, tt


def _layer_spec(stacked, layer):
    zeros = (0,) * (stacked.ndim - 1)
    return pl.BlockSpec((None,) + stacked.shape[1:], lambda b, t: (layer,) + zeros, pipeline_mode=pl.Buffered(1))


def _state_spec(stacked_shape, layer, nb):
    zeros = (0,) * (len(stacked_shape) - 2)
    return pl.BlockSpec((None, nb) + tuple(stacked_shape[2:]), lambda b, t: (layer, b) + zeros)


def _per_batch(shape, nb):
    return pl.BlockSpec((nb,) + tuple(shape[1:]), lambda b, t: (b,) + (0,) * (len(shape) - 1))


MIXER_PARAMS = ('g_pre', 'w_qkvo', 'w_if', 'w_ift', 'b_if_row', 'b_if_col', 'mnorm', 'w_pa', 'w_u', 'conv_w',
                'conv_b', 'ln_g', 'ln_b', 'w_pb', 'b_pb', 'w_g', 'b_merge', 'w_out', 'g_post')
FFN_PARAMS = ('g_pre', 'w_up', 'dw', 'dw_b', 'w_down', 'g_post')


def _mixer_layer(x, init, p, layer, memory_stack):
    batch, seq, d_model = x.shape
    d_a = p['mnorm'].shape[-1]
    dh = d_a // NUM_HEADS
    d_b = p['conv_b'].shape[-1]
    conv_w = p['conv_w'].shape[-2]
    has_init = init is not None
    memory_bytes = NUM_HEADS * dh * dh * 4 * (4 if has_init else 2)
    nb, tt = _block_rows(batch, seq, MIXER_BLOCK_ROWS, memory_bytes)
    x_spec = pl.BlockSpec((nb, tt, d_model), lambda b, t: (b, t, 0))
    weights = [p[k] for k in MIXER_PARAMS]
    state_shapes = [(batch, NUM_HEADS, dh, dh), (batch, NUM_HEADS, dh), (batch, NUM_HEADS, LANES),
                    (batch, conv_w - 1, d_b)]
    operands = [x] + (list(init) if has_init else []) + weights + [memory_stack]
    in_specs = ([x_spec] + ([_state_spec(s.shape, layer, nb) for s in init] if has_init else [])
                + [_layer_spec(w, layer) for w in weights] + [pl.BlockSpec(memory_space=pl.ANY)])
    out_shape = ([jax.ShapeDtypeStruct(x.shape, F32), jax.ShapeDtypeStruct(memory_stack.shape, F32)]
                 + [jax.ShapeDtypeStruct(s, F32) for s in state_shapes[1:]])
    out_specs = ([x_spec, _state_spec(memory_stack.shape, layer, nb)]
                 + [_per_batch(s, nb) for s in state_shapes[1:]])
    rows = nb * tt
    return pl.pallas_call(
        functools.partial(_mixer_kernel, nb=nb, tt=tt, has_init=has_init),
        grid=(batch // nb, seq // tt),
        in_specs=in_specs, out_specs=out_specs, out_shape=out_shape,
        input_output_aliases={len(operands) - 1: 1},
        scratch_shapes=[pltpu.VMEM((rows, d_model), BF16), pltpu.VMEM((rows, d_a), BF16),
                        pltpu.VMEM((nb, CONV_HIST_ROWS + tt, d_b), F32),
                        pltpu.VMEM((rows, d_b), F32), pltpu.VMEM((rows, 2 * d_model), F32),
                        pltpu.VMEM((nb, 2, tt, LANES), F32),
                        pltpu.VMEM((nb, 2, SUBLANES, tt), F32)],
        compiler_params=pltpu.CompilerParams(dimension_semantics=("arbitrary", "arbitrary"),
                                             vmem_limit_bytes=VMEM_LIMIT_BYTES),
        name="mixer_init" if has_init else "mixer",
    )(*operands)


def _ffn_layer(x, init, p, layer):
    batch, seq, d_model = x.shape
    nb, tt = _block_rows(batch, seq, FFN_BLOCK_ROWS)
    d_ff = p['dw'].shape[-1] // 2
    width = p['dw'].shape[-2]
    has_init = init is not None
    x_spec = pl.BlockSpec((nb, tt, d_model), lambda b, t: (b, t, 0))
    weights = [p[k] for k in FFN_PARAMS]
    fb_shape = (batch, width - 1, 2 * d_ff)
    operands = [x] + ([init] if has_init else []) + weights
    in_specs = ([x_spec] + ([_state_spec(init.shape, layer, nb)] if has_init else [])
                + [_layer_spec(w, layer) for w in weights])
    return pl.pallas_call(
        functools.partial(_ffn_kernel, nb=nb, tt=tt, has_init=has_init),
        grid=(batch // nb, seq // tt),
        in_specs=in_specs,
        out_specs=[x_spec, _per_batch(fb_shape, nb)],
        out_shape=[jax.ShapeDtypeStruct(x.shape, F32), jax.ShapeDtypeStruct(fb_shape, F32)],
        scratch_shapes=[pltpu.VMEM((nb, FFN_HIST_ROWS, 2 * d_ff), F32)],
        compiler_params=pltpu.CompilerParams(dimension_semantics=("arbitrary", "arbitrary"),
                                             vmem_limit_bytes=VMEM_LIMIT_BYTES),
        name="ffn_init" if has_init else "ffn",
    )(*operands)


def _pack_kernel(w_ref, o_ref):
    o_ref[...] = pltpu.bitcast(w_ref[...].astype(BF16), jnp.uint32)


def _pack_weights(w):
    depth, k, n = w.shape
    tn = max(c for c in range(LANES, n + 1, LANES) if n % c == 0 and k * c * w.dtype.itemsize <= PACK_BLOCK_BYTES)
    return pl.pallas_call(
        _pack_kernel,
        grid=(depth, n // tn),
        in_specs=[pl.BlockSpec((None, k, tn), lambda l, j: (l, 0, j))],
        out_specs=pl.BlockSpec((None, k // 2, tn), lambda l, j: (l, 0, j)),
        out_shape=jax.ShapeDtypeStruct((depth, k // 2, n), jnp.uint32),
        name="pack_weights",
    )(w)


def _rows(v):
    return v.reshape(v.shape[0], 1, v.shape[1]).astype(F32)


def _param_stacks(norm_mix_pre, norm_mix_post, norm_ffn_pre, norm_ffn_post, w_in, b_i, b_f, mlstm_norm, w_proj_a,
                  conv_dw, conv_b, conv_ln_g, conv_ln_b, w_proj_b, b_proj_b, b_merge, w_out, w_up, ffn_dw,
                  ffn_dw_b, w_down):
    depth = w_in.shape[0]
    d_a = w_proj_a.shape[1]
    d_b = w_proj_b.shape[1]
    nh = b_i.shape[1]
    o_if = 4 * d_a
    o_u = o_if + 2 * nh
    o_g = o_u + 2 * d_b
    w_if = w_in[:, :, o_if:o_if + LANES].astype(BF16)
    b_if = jnp.concatenate([b_i, b_f], axis=1).astype(F32)
    mixer = {
        'g_pre': _rows(norm_mix_pre),
        'w_qkvo': _pack_weights(w_in[:, :, :o_if].astype(BF16)),
        'w_if': w_if,
        'w_ift': jnp.swapaxes(w_if, 1, 2)[:, :2 * nh, :],
        'b_if_row': jnp.pad(b_if, ((0, 0), (0, LANES - 2 * nh))).reshape(depth, 1, LANES),
        'b_if_col': jnp.broadcast_to(b_if[:, :, None], (depth, 2 * nh, LANES)),
        'mnorm': _rows(mlstm_norm),
        'w_pa': _pack_weights(w_proj_a),
        'w_u': _pack_weights(w_in[:, :, o_u:o_g].astype(BF16)),
        'conv_w': conv_dw.astype(F32),
        'conv_b': _rows(conv_b),
        'ln_g': _rows(conv_ln_g),
        'ln_b': _rows(conv_ln_b),
        'w_pb': _pack_weights(w_proj_b),
        'b_pb': _rows(b_proj_b),
        'w_g': _pack_weights(w_in[:, :, o_g:].astype(BF16)),
        'b_merge': _rows(b_merge),
        'w_out': _pack_weights(w_out),
        'g_post': _rows(norm_mix_post),
    }
    ffn = {
        'g_pre': _rows(norm_ffn_pre),
        'w_up': _pack_weights(w_up),
        'dw': ffn_dw.astype(F32),
        'dw_b': _rows(ffn_dw_b),
        'w_down': _pack_weights(w_down),
        'g_post': _rows(norm_ffn_post),
    }
    return mixer, ffn


def kernel(x_prompt, x_sample, state_mlstm_C, state_mlstm_n, state_mlstm_m, cache_conv, cache_ffn_conv, norm_mix_pre, norm_mix_post, norm_ffn_pre, norm_ffn_post, w_in, b_i, b_f, mlstm_norm, w_proj_a, conv_dw, conv_b, conv_ln_g, conv_ln_b, w_proj_b, b_proj_b, b_merge, w_out, w_up, ffn_dw, ffn_dw_b, w_down):
    assert b_i.shape[1] == NUM_HEADS
    depth = w_in.shape[0]
    mixer_p, ffn_p = _param_stacks(norm_mix_pre, norm_mix_post, norm_ffn_pre, norm_ffn_post, w_in, b_i, b_f,
                                   mlstm_norm, w_proj_a, conv_dw, conv_b, conv_ln_g, conv_ln_b, w_proj_b,
                                   b_proj_b, b_merge, w_out, w_up, ffn_dw, ffn_dw_b, w_down)
    m0 = jnp.broadcast_to(state_mlstm_m[..., None], state_mlstm_m.shape + (LANES,)).astype(F32)
    sample_init = (state_mlstm_C, state_mlstm_n, m0, cache_conv)
    yp, ys = x_prompt, x_sample
    pc = jnp.zeros((depth, x_prompt.shape[0]) + state_mlstm_C.shape[2:], F32)
    sc = jnp.zeros(state_mlstm_C.shape, F32)
    prompt_states, sample_states = [], []
    for l in range(depth):
        yp, pc, n1, m1, cb1 = _mixer_layer(yp, None, mixer_p, l, pc)
        yp, fb1 = _ffn_layer(yp, None, ffn_p, l)
        prompt_states.append((n1, m1, cb1, fb1))
        ys, sc, n2, m2, cb2 = _mixer_layer(ys, sample_init, mixer_p, l, sc)
        ys, fb2 = _ffn_layer(ys, cache_ffn_conv, ffn_p, l)
        sample_states.append((n2, m2, cb2, fb2))
    pn, pm, pcb, pfb = (jnp.stack(s) for s in zip(*prompt_states))
    sn, sm, scb, sfb = (jnp.stack(s) for s in zip(*sample_states))
    return (yp, ys, pc, pn, pm[..., 0], pcb, pfb, sc, sn, sm[..., 0], scb, sfb)
```

```python
import functools
import math

import jax
import jax.numpy as jnp
from jax import lax
from jax.experimental import pallas as pl
from jax.experimental.pallas import tpu as pltpu

EPS = 1e-6
LOG2_E = 1.0 / math.log(2.0)
NUM_HEADS = 4
SUBLANES = 8
LANES = 128
CONV_HIST_ROWS = 32
FFN_HIST_ROWS = 8
BLOCK_POSITIONS = 256
MIXER_BLOCK_ROWS = 256
FFN_BLOCK_ROWS = 512
CONV_PART_ROWS = 128
VMEM_LIMIT_BYTES = 56 * 1024 * 1024
STATE_VMEM_BYTES = 8 * 1024 * 1024
PACK_BLOCK_BYTES = 6 * 1024 * 1024

F32 = jnp.float32
BF16 = jnp.bfloat16


def _dot(a, b):
    return jnp.dot(a, b, preferred_element_type=F32)


def _wdot(a, w_ref, rows=slice(None), cols=slice(None)):
    return _dot(a, pltpu.bitcast(w_ref[rows, cols], BF16))


def _dot_nt(a, b):
    return lax.dot_general(a, b, (((1,), (1,)), ((), ())), preferred_element_type=F32)


def _dot_tn(a, b):
    return lax.dot_general(a, b, (((0,), (0,)), ((), ())), preferred_element_type=F32)


def _sigmoid(x):
    return 1.0 / (1.0 + jnp.exp2(x * (-LOG2_E)))


def _log_sigmoid(x):
    return jnp.minimum(x, 0.0) - jnp.log1p(jnp.exp(-jnp.abs(x)))


def _rms_scale(x):
    return x * lax.rsqrt(jnp.mean(x * x, axis=-1, keepdims=True) + EPS)


def _split_bf16(x):
    hi = x.astype(BF16)
    lo = (x - hi.astype(F32)).astype(BF16)
    return hi, lo


def _mlstm_gates(xb, w_if_ref, w_ift_ref, b_if_row_ref, b_if_col_ref, tril, triu):
    g_col = _dot(xb, w_if_ref[...]) + b_if_row_ref[...]
    hi, lo = _split_bf16(_log_sigmoid(g_col))
    bcum_col = _dot(tril, hi) + _dot(tril, lo)
    rows_used = w_ift_ref.shape[0]
    if xb.shape[0] % LANES == 0:
        g_row = jnp.transpose(g_col)[:rows_used, :]
        bcum_row = jnp.transpose(bcum_col)[:rows_used, :]
    else:
        g_row = _dot_nt(w_ift_ref[...], xb) + b_if_col_ref[:, 0:1]
        hi, lo = _split_bf16(_log_sigmoid(g_row))
        bcum_row = _dot(hi, triu) + _dot(lo, triu)
    return g_col, bcum_col, g_row, bcum_row


def _mlstm_block(q, k, v, gates, h, causal, c_prev, n_prev, m_prev):
    tt = q.shape[0]
    g_col, bcum_col, g_row, bcum_row = gates
    ig_c = g_col[:, h:h + 1]
    bc = bcum_col[:, NUM_HEADS + h:NUM_HEADS + h + 1]
    ig_r = g_row[h:h + 1, :]
    br = bcum_row[NUM_HEADS + h:NUM_HEADS + h + 1, :]
    qb, kb, vb = q.astype(BF16), k.astype(BF16), v.astype(BF16)

    logd = jnp.where(causal, (bc - br) + ig_r, -jnp.inf)
    g = bc + m_prev
    m_tok = jnp.maximum(g, jnp.max(logd, axis=-1, keepdims=True))
    w = jnp.exp(logd - m_tok)
    inter = jnp.exp(g - m_tok)
    s = _dot_nt(qb, kb) * w
    num = _dot(s.astype(BF16), vb) + inter * _dot(qb, c_prev.astype(BF16))
    den = jnp.sum(s, axis=-1, keepdims=True) + inter * jnp.sum(q * n_prev, axis=-1, keepdims=True)
    hh = num * (1.0 / jnp.maximum(jnp.abs(den), jnp.exp(-m_tok)))

    m_new = m_tok[tt - 1:tt, :]
    decay = inter[tt - 1:tt, :]
    w_last = jnp.exp((bc[tt - 1:tt, :] - bc) + ig_c - m_new)
    c_new = decay * c_prev + _dot_tn(kb, (w_last * v).astype(BF16))
    n_new = decay * n_prev + jnp.sum(w_last * k, axis=0, keepdims=True)
    return hh, c_new, n_new, m_new


def _causal_conv_tile(glu_scr, conv_scr, conv_w_ref, conv_b_ref, b, tt, ls, hist0):
    conv_w = conv_w_ref.shape[0]
    part = min(tt, CONV_PART_ROWS)
    span = part + CONV_HIST_ROWS
    for p0 in range(0, tt, part):
        acc = jnp.broadcast_to(conv_b_ref[:, ls], (part, LANES))
        rows_in = glu_scr[b, p0:p0 + span, ls]
        for r in range(SUBLANES):
            offs = [o for o in range(hist0, hist0 + conv_w) if o % SUBLANES == r]
            window = rows_in if r == 0 else pltpu.roll(rows_in, span - r, axis=0)
            for o in offs:
                kk = o - hist0
                acc = acc + conv_w_ref[kk:kk + 1, ls] * window[o - r:o - r + part]
        conv_scr[b * tt + p0:b * tt + p0 + part, ls] = acc


def _mixer_kernel(*refs, nb, tt, has_init):
    if has_init:
        (x_ref, c0_ref, n0_ref, m0_ref, cb0_ref, *rest) = refs
    else:
        (x_ref, *rest) = refs
    (g_pre_ref, w_qkvo_ref, w_if_ref, w_ift_ref, b_if_row_ref, b_if_col_ref, mnorm_ref, w_pa_ref,
     w_u_ref, conv_w_ref, conv_b_ref, ln_g_ref, ln_b_ref, w_pb_ref, b_pb_ref, w_g_ref, b_merge_ref,
     w_out_ref, g_post_ref, _memory_stack_ref,
     y_ref, c_ref, n_ref, m_ref, cb_ref,
     xn_scr, hg_scr, glu_scr, conv_scr, gt_scr, gcol_scr, grow_scr) = rest

    t = pl.program_id(1)
    rows = nb * tt
    d_model = x_ref.shape[-1]
    d_a = mnorm_ref.shape[1]
    dh = d_a // NUM_HEADS
    d_b = conv_b_ref.shape[1]
    conv_w = conv_w_ref.shape[0]
    hist0 = CONV_HIST_ROWS - (conv_w - 1)

    @pl.when(t == 0)
    def _init_state():
        glu_scr[:, 0:CONV_HIST_ROWS, :] = jnp.zeros((nb, CONV_HIST_ROWS, d_b), F32)
        if has_init:
            c_ref[...] = c0_ref[...]
            n_ref[...] = n0_ref[...]
            m_ref[...] = m0_ref[...]
            glu_scr[:, hist0:CONV_HIST_ROWS, :] = cb0_ref[...]
        else:
            c_ref[...] = jnp.zeros(c_ref.shape, F32)
            n_ref[...] = jnp.zeros(n_ref.shape, F32)
            m_ref[...] = jnp.zeros(m_ref.shape, F32)

    x = x_ref[...].reshape(rows, d_model)
    xn_scr[...] = (_rms_scale(x) * g_pre_ref[...]).astype(BF16)

    row_id = lax.broadcasted_iota(jnp.int32, (tt, tt), 0)
    col_id = lax.broadcasted_iota(jnp.int32, (tt, tt), 1)
    tril = jnp.where(row_id >= col_id, 1.0, 0.0).astype(BF16)
    triu = jnp.where(col_id >= row_id, 1.0, 0.0).astype(BF16)
    for b in range(nb):
        g_col, bcum_col, g_row, bcum_row = _mlstm_gates(
            xn_scr[b * tt:(b + 1) * tt, :], w_if_ref, w_ift_ref, b_if_row_ref, b_if_col_ref, tril, triu)
        gcol_scr[b, 0], gcol_scr[b, 1] = g_col, bcum_col
        grow_scr[b, 0], grow_scr[b, 1] = g_row, bcum_row

    xn = xn_scr[...]
    u = _wdot(xn, w_u_ref)
    glu_scr[:, CONV_HIST_ROWS:CONV_HIST_ROWS + tt, :] = (u[:, :d_b] * _sigmoid(u[:, d_b:])).reshape(nb, tt, d_b)

    dg = d_model // NUM_HEADS
    causal = row_id >= col_id
    lane_tiles = d_b // LANES
    for h in range(NUM_HEADS):
        for j in range(h * lane_tiles // NUM_HEADS, (h + 1) * lane_tiles // NUM_HEADS):
            for b in range(nb):
                _causal_conv_tile(glu_scr, conv_scr, conv_w_ref, conv_b_ref, b, tt,
                                  slice(j * LANES, (j + 1) * LANES), hist0)
        hs = slice(h * dh, (h + 1) * dh)
        q_all = _wdot(xn, w_qkvo_ref, cols=slice(h * dh, (h + 1) * dh))
        k_all = _wdot(xn, w_qkvo_ref, cols=slice(d_a + h * dh, d_a + (h + 1) * dh)) * (1.0 / math.sqrt(dh))
        v_all = _wdot(xn, w_qkvo_ref, cols=slice(2 * d_a + h * dh, 2 * d_a + (h + 1) * dh))
        o_all = _wdot(xn, w_qkvo_ref, cols=slice(3 * d_a + h * dh, 3 * d_a + (h + 1) * dh))
        for b in range(nb):
            rs = slice(b * tt, (b + 1) * tt)
            gates = (gcol_scr[b, 0], gcol_scr[b, 1], grow_scr[b, 0], grow_scr[b, 1])
            hh, c_new, n_new, m_new = _mlstm_block(
                q_all[rs], k_all[rs], v_all[rs], gates, h, causal,
                c_ref[b, h], n_ref[b, h:h + 1, :], m_ref[b, h:h + 1, 0:1])
            c_ref[b, h] = c_new
            n_ref[b, h:h + 1, :] = n_new
            m_ref[b, h:h + 1, :] = jnp.broadcast_to(m_new, (1, LANES))
            hn = _rms_scale(hh) * mnorm_ref[:, hs]
            hg_scr[rs, hs] = (hn * _sigmoid(o_all[rs])).astype(BF16)

        for half in range(2):
            gs = slice(half * d_model + h * dg, half * d_model + (h + 1) * dg)
            gt_scr[:, gs] = _sigmoid(_wdot(xn, w_g_ref, cols=gs) + b_merge_ref[:, gs])
    glu_scr[:, 0:CONV_HIST_ROWS, :] = glu_scr[:, tt:tt + CONV_HIST_ROWS, :]

    if nb % 2 == 0:
        splits = [(slice(i * nb // 2, (i + 1) * nb // 2), slice(0, tt)) for i in range(2)]
    elif nb == 1 and tt % (4 * SUBLANES) == 0:
        splits = [(slice(0, 1), slice(i * tt // 2, (i + 1) * tt // 2)) for i in range(2)]
    else:
        splits = [(slice(0, nb), slice(0, tt))]
    part_rows = rows // len(splits)
    for i, (bs, ts) in enumerate(splits):
        rs = slice(i * part_rows, (i + 1) * part_rows)
        c = conv_scr[rs, :]
        xc = c - jnp.mean(c, axis=-1, keepdims=True)
        cn = xc * lax.rsqrt(jnp.mean(xc * xc, axis=-1, keepdims=True) + EPS) * ln_g_ref[...] + ln_b_ref[...]
        act = cn * _sigmoid(cn)
        yb = _wdot(act.astype(BF16), w_pb_ref) + b_pb_ref[...]
        ya = _wdot(hg_scr[rs, :], w_pa_ref)
        mix = gt_scr[rs, :d_model] * ya + gt_scr[rs, d_model:] * yb
        mo = _wdot(mix.astype(BF16), w_out_ref)
        y = x_ref[bs, ts, :].reshape(part_rows, d_model) + _rms_scale(mo) * g_post_ref[...]
        y_ref[bs, ts, :] = y.reshape(bs.stop - bs.start, ts.stop - ts.start, d_model)

    @pl.when(t == pl.num_programs(1) - 1)
    def _emit_conv_state():
        cb_ref[...] = glu_scr[:, hist0:CONV_HIST_ROWS, :]


def _ffn_kernel(*refs, nb, tt, has_init):
    if has_init:
        (x_ref, fb0_ref, *rest) = refs
    else:
        (x_ref, *rest) = refs
    (g_pre_ref, w_up_ref, dw_ref, dw_b_ref, w_down_ref, g_post_ref,
     y_ref, fb_ref, hist_scr) = rest

    t = pl.program_id(1)
    rows = nb * tt
    d_model = x_ref.shape[-1]
    d_ff = dw_ref.shape[1] // 2
    width = dw_ref.shape[0]
    hist0 = FFN_HIST_ROWS - (width - 1)

    @pl.when(t == 0)
    def _init_state():
        hist_scr[...] = jnp.zeros(hist_scr.shape, F32)
        if has_init:
            hist_scr[:, hist0:FFN_HIST_ROWS, :] = fb0_ref[...]

    x = x_ref[...].reshape(rows, d_model)
    hn = (_rms_scale(x) * g_pre_ref[...]).astype(BF16)
    up = _wdot(hn, w_up_ref)

    sub_row = lax.broadcasted_iota(jnp.int32, (SUBLANES, 2 * d_ff), 0)
    conv = dw_ref[width - 1:width, :] * up + dw_b_ref[...]
    for kk in range(width - 1):
        shift = width - 1 - kk
        moved = pltpu.roll(up, shift, axis=0)
        pieces = []
        for b in range(nb):
            head = jnp.where(sub_row < shift, pltpu.roll(hist_scr[b], shift, axis=0),
                             moved[b * tt:b * tt + SUBLANES])
            pieces += [head, moved[b * tt + SUBLANES:(b + 1) * tt]]
        conv = conv + dw_ref[kk:kk + 1, :] * jnp.concatenate(pieces, axis=0)
    for b in range(nb):
        hist_scr[b] = up[(b + 1) * tt - FFN_HIST_ROWS:(b + 1) * tt]

    lin = -2.0 * LOG2_E * math.sqrt(2.0 / math.pi)
    halves = 2 if nb % 2 == 0 else 1
    hb = nb // halves
    for i in range(halves):
        rs = slice(i * hb * tt, (i + 1) * hb * tt)
        gate = conv[rs, :d_ff]
        cdf = 1.0 / (1.0 + jnp.exp2(gate * (lin + (lin * 0.044715) * (gate * gate))))
        f = _wdot((gate * conv[rs, d_ff:] * cdf).astype(BF16), w_down_ref)
        y = x_ref[i * hb:(i + 1) * hb].reshape(hb * tt, d_model) + _rms_scale(f) * g_post_ref[...]
        y_ref[i * hb:(i + 1) * hb] = y.reshape(hb, tt, d_model)

    @pl.when(t == pl.num_programs(1) - 1)
    def _emit_conv_state():
        fb_ref[...] = hist_scr[:, hist0:FFN_HIST_ROWS, :]


def _block_rows(batch, seq, max_rows, bytes_per_batch_row=0):
    tt = BLOCK_POSITIONS if seq % BLOCK_POSITIONS == 0 else seq
    assert tt % (2 * SUBLANES) == 0 and tt >= CONV_HIST_ROWS, (batch, seq)
    nb = max(1, min(batch, max_rows // tt))
    if bytes_per_batch_row:
        nb = max(1, min(nb, STATE_VMEM_BYTES // bytes_per_batch_row))
    while batch % nb:
        nb -= 1
    return nb, tt


def _layer_spec(stacked, layer):
    zeros = (0,) * (stacked.ndim - 1)
    return pl.BlockSpec((None,) + stacked.shape[1:], lambda b, t: (layer,) + zeros, pipeline_mode=pl.Buffered(1))


def _state_spec(stacked_shape, layer, nb):
    zeros = (0,) * (len(stacked_shape) - 2)
    return pl.BlockSpec((None, nb) + tuple(stacked_shape[2:]), lambda b, t: (layer, b) + zeros)


def _per_batch(shape, nb):
    return pl.BlockSpec((nb,) + tuple(shape[1:]), lambda b, t: (b,) + (0,) * (len(shape) - 1))


MIXER_PARAMS = ('g_pre', 'w_qkvo', 'w_if', 'w_ift', 'b_if_row', 'b_if_col', 'mnorm', 'w_pa', 'w_u', 'conv_w',
                'conv_b', 'ln_g', 'ln_b', 'w_pb', 'b_pb', 'w_g', 'b_merge', 'w_out', 'g_post')
FFN_PARAMS = ('g_pre', 'w_up', 'dw', 'dw_b', 'w_down', 'g_post')


def _mixer_layer(x, init, p, layer, memory_stack):
    batch, seq, d_model = x.shape
    d_a = p['mnorm'].shape[-1]
    dh = d_a // NUM_HEADS
    d_b = p['conv_b'].shape[-1]
    conv_w = p['conv_w'].shape[-2]
    has_init = init is not None
    memory_bytes = NUM_HEADS * dh * dh * 4 * (4 if has_init else 2)
    nb, tt = _block_rows(batch, seq, MIXER_BLOCK_ROWS, memory_bytes)
    x_spec = pl.BlockSpec((nb, tt, d_model), lambda b, t: (b, t, 0))
    weights = [p[k] for k in MIXER_PARAMS]
    state_shapes = [(batch, NUM_HEADS, dh, dh), (batch, NUM_HEADS, dh), (batch, NUM_HEADS, LANES),
                    (batch, conv_w - 1, d_b)]
    operands = [x] + (list(init) if has_init else []) + weights + [memory_stack]
    in_specs = ([x_spec] + ([_state_spec(s.shape, layer, nb) for s in init] if has_init else [])
                + [_layer_spec(w, layer) for w in weights] + [pl.BlockSpec(memory_space=pl.ANY)])
    out_shape = ([jax.ShapeDtypeStruct(x.shape, F32), jax.ShapeDtypeStruct(memory_stack.shape, F32)]
                 + [jax.ShapeDtypeStruct(s, F32) for s in state_shapes[1:]])
    out_specs = ([x_spec, _state_spec(memory_stack.shape, layer, nb)]
                 + [_per_batch(s, nb) for s in state_shapes[1:]])
    rows = nb * tt
    return pl.pallas_call(
        functools.partial(_mixer_kernel, nb=nb, tt=tt, has_init=has_init),
        grid=(batch // nb, seq // tt),
        in_specs=in_specs, out_specs=out_specs, out_shape=out_shape,
        input_output_aliases={len(operands) - 1: 1},
        scratch_shapes=[pltpu.VMEM((rows, d_model), BF16), pltpu.VMEM((rows, d_a), BF16),
                        pltpu.VMEM((nb, CONV_HIST_ROWS + tt, d_b), F32),
                        pltpu.VMEM((rows, d_b), F32), pltpu.VMEM((rows, 2 * d_model), F32),
                        pltpu.VMEM((nb, 2, tt, LANES), F32),
                        pltpu.VMEM((nb, 2, SUBLANES, tt), F32)],
        compiler_params=pltpu.CompilerParams(dimension_semantics=("arbitrary", "arbitrary"),
                                             vmem_limit_bytes=VMEM_LIMIT_BYTES),
        name="mixer_init" if has_init else "mixer",
    )(*operands)


def _ffn_layer(x, init, p, layer):
    batch, seq, d_model = x.shape
    nb, tt = _block_rows(batch, seq, FFN_BLOCK_ROWS)
    d_ff = p['dw'].shape[-1] // 2
    width = p['dw'].shape[-2]
    has_init = init is not None
    x_spec = pl.BlockSpec((nb, tt, d_model), lambda b, t: (b, t, 0))
    weights = [p[k] for k in FFN_PARAMS]
    fb_shape = (batch, width - 1, 2 * d_ff)
    operands = [x] + ([init] if has_init else []) + weights
    in_specs = ([x_spec] + ([_state_spec(init.shape, layer, nb)] if has_init else [])
                + [_layer_spec(w, layer) for w in weights])
    return pl.pallas_call(
        functools.partial(_ffn_kernel, nb=nb, tt=tt, has_init=has_init),
        grid=(batch // nb, seq // tt),
        in_specs=in_specs,
        out_specs=[x_spec, _per_batch(fb_shape, nb)],
        out_shape=[jax.ShapeDtypeStruct(x.shape, F32), jax.ShapeDtypeStruct(fb_shape, F32)],
        scratch_shapes=[pltpu.VMEM((nb, FFN_HIST_ROWS, 2 * d_ff), F32)],
        compiler_params=pltpu.CompilerParams(dimension_semantics=("arbitrary", "arbitrary"),
                                             vmem_limit_bytes=VMEM_LIMIT_BYTES),
        name="ffn_init" if has_init else "ffn",
    )(*operands)


def _pack_kernel(w_ref, o_ref):
    o_ref[...] = pltpu.bitcast(w_ref[...].astype(BF16), jnp.uint32)


def _pack_weights(w):
    depth, k, n = w.shape
    tn = max(c for c in range(LANES, n + 1, LANES) if n % c == 0 and k * c * w.dtype.itemsize <= PACK_BLOCK_BYTES)
    return pl.pallas_call(
        _pack_kernel,
        grid=(depth, n // tn),
        in_specs=[pl.BlockSpec((None, k, tn), lambda l, j: (l, 0, j))],
        out_specs=pl.BlockSpec((None, k // 2, tn), lambda l, j: (l, 0, j)),
        out_shape=jax.ShapeDtypeStruct((depth, k // 2, n), jnp.uint32),
        name="pack_weights",
    )(w)


def _rows(v):
    return v.reshape(v.shape[0], 1, v.shape[1]).astype(F32)


def _param_stacks(norm_mix_pre, norm_mix_post, norm_ffn_pre, norm_ffn_post, w_in, b_i, b_f, mlstm_norm, w_proj_a,
                  conv_dw, conv_b, conv_ln_g, conv_ln_b, w_proj_b, b_proj_b, b_merge, w_out, w_up, ffn_dw,
                  ffn_dw_b, w_down):
    depth = w_in.shape[0]
    d_a = w_proj_a.shape[1]
    d_b = w_proj_b.shape[1]
    nh = b_i.shape[1]
    o_if = 4 * d_a
    o_u = o_if + 2 * nh
    o_g = o_u + 2 * d_b
    w_if = w_in[:, :, o_if:o_if + LANES].astype(BF16)
    b_if = jnp.concatenate([b_i, b_f], axis=1).astype(F32)
    mixer = {
        'g_pre': _rows(norm_mix_pre),
        'w_qkvo': _pack_weights(w_in[:, :, :o_if].astype(BF16)),
        'w_if': w_if,
        'w_ift': jnp.swapaxes(w_if, 1, 2)[:, :2 * nh, :],
        'b_if_row': jnp.pad(b_if, ((0, 0), (0, LANES - 2 * nh))).reshape(depth, 1, LANES),
        'b_if_col': jnp.broadcast_to(b_if[:, :, None], (depth, 2 * nh, LANES)),
        'mnorm': _rows(mlstm_norm),
        'w_pa': _pack_weights(w_proj_a),
        'w_u': _pack_weights(w_in[:, :, o_u:o_g].astype(BF16)),
        'conv_w': conv_dw.astype(F32),
        'conv_b': _rows(conv_b),
        'ln_g': _rows(conv_ln_g),
        'ln_b': _rows(conv_ln_b),
        'w_pb': _pack_weights(w_proj_b),
        'b_pb': _rows(b_proj_b),
        'w_g': _pack_weights(w_in[:, :, o_g:].astype(BF16)),
        'b_merge': _rows(b_merge),
        'w_out': _pack_weights(w_out),
        'g_post': _rows(norm_mix_post),
    }
    ffn = {
        'g_pre': _rows(norm_ffn_pre),
        'w_up': _pack_weights(w_up),
        'dw': ffn_dw.astype(F32),
        'dw_b': _rows(ffn_dw_b),
        'w_down': _pack_weights(w_down),
        'g_post': _rows(norm_ffn_post),
    }
    return mixer, ffn


def kernel(x_prompt, x_sample, state_mlstm_C, state_mlstm_n, state_mlstm_m, cache_conv, cache_ffn_conv, norm_mix_pre, norm_mix_post, norm_ffn_pre, norm_ffn_post, w_in, b_i, b_f, mlstm_norm, w_proj_a, conv_dw, conv_b, conv_ln_g, conv_ln_b, w_proj_b, b_proj_b, b_merge, w_out, w_up, ffn_dw, ffn_dw_b, w_down):
    assert b_i.shape[1] == NUM_HEADS
    depth = w_in.shape[0]
    mixer_p, ffn_p = _param_stacks(norm_mix_pre, norm_mix_post, norm_ffn_pre, norm_ffn_post, w_in, b_i, b_f,
                                   mlstm_norm, w_proj_a, conv_dw, conv_b, conv_ln_g, conv_ln_b, w_proj_b,
                                   b_proj_b, b_merge, w_out, w_up, ffn_dw, ffn_dw_b, w_down)
    m0 = jnp.broadcast_to(state_mlstm_m[..., None], state_mlstm_m.shape + (LANES,)).astype(F32)
    sample_init = (state_mlstm_C, state_mlstm_n, m0, cache_conv)
    yp, ys = x_prompt, x_sample
    pc = jnp.zeros((depth, x_prompt.shape[0]) + state_mlstm_C.shape[2:], F32)
    sc = jnp.zeros(state_mlstm_C.shape, F32)
    prompt_states, sample_states = [], []
    for l in range(depth):
        yp, pc, n1, m1, cb1 = _mixer_layer(yp, None, mixer_p, l, pc)
        yp, fb1 = _ffn_layer(yp, None, ffn_p, l)
        prompt_states.append((n1, m1, cb1, fb1))
        ys, sc, n2, m2, cb2 = _mixer_layer(ys, sample_init, mixer_p, l, sc)
        ys, fb2 = _ffn_layer(ys, cache_ffn_conv, ffn_p, l)
        sample_states.append((n2, m2, cb2, fb2))
    pn, pm, pcb, pfb = (jnp.stack(s) for s in zip(*prompt_states))
    sn, sm, scb, sfb = (jnp.stack(s) for s in zip(*sample_states))
    return (yp, ys, pc, pn, pm[..., 0], pcb, pfb, sc, sn, sm[..., 0], scb, sfb)
```
